```python
import math
import functools
import jax
import jax.numpy as jnp
from jax import lax
import numpy as np

D_MODEL = 2048
BATCH = 16
SEQ = 2048
DEPTH = 1
DEC_BATCH = 32
DEC_SEQ = 4
PAST_LEN = 16384
PAGE_SIZE = 128

ATTN_WIDTH = D_MODEL // 2
LRU_WIDTH = D_MODEL - ATTN_WIDTH
HEAD_DIM = 64
N_HEADS = ATTN_WIDTH // (2 * HEAD_DIM)
LRU_BLOCKS = 8
LRU_BLOCK_W = LRU_WIDTH // LRU_BLOCKS
CONV_W = 4
LRU_C = 8.0
REL_BUCKETS = 32
REL_EXACT = REL_BUCKETS // 2
REL_MAX_DIST = 128
N_EXPERTS = 32
TOP_K = 4
D_FF = D_MODEL
SWIGLU_LIMIT = 7.0
SWIGLU_ALPHA = 1.702
MOE_BLOCK = 256
Q_BLOCK = 128
EPS = 1e-5
NEG_INF = -1e30
PROJ_WIDTH = 3 * ATTN_WIDTH + 2 * LRU_WIDTH

kernel_name = 'hymba_diffattn_rglru_moe_step'


def _rmsnorm(x, g):
    xf = x.astype(jnp.float32)
    y = xf * lax.rsqrt(jnp.mean(xf * xf, axis=-1, keepdims=True) + EPS)
    return (y * g.astype(jnp.float32)).astype(x.dtype)


def _lambda_init(layer):
    return 0.8 - 0.6 * math.exp(-0.3 * layer)


def _heads_qk(rows):
    b, s = rows.shape[:2]
    return rows.reshape(b, s, N_HEADS, 2, HEAD_DIM).transpose(0, 2, 3, 1, 4)


def _rel_bias(rel_table, qpos, kpos):
    n = jnp.maximum(qpos[:, None] - kpos[None, :], 0)
    nf = jnp.maximum(n, 1).astype(jnp.float32)
    large = REL_EXACT + (jnp.log(nf / REL_EXACT) / math.log(REL_MAX_DIST / REL_EXACT)
                         * (REL_BUCKETS - REL_EXACT)).astype(jnp.int32)
    bucket = jnp.where(n < REL_EXACT, n, jnp.minimum(large, REL_BUCKETS - 1))
    return jnp.transpose(rel_table[bucket].astype(jnp.float32), (2, 0, 1))


def _attn_partial(qh, kh, vh, bias, mask):
    s = jnp.einsum('bhmqd,bhmkd->bhmqk', qh.astype(jnp.float32), kh.astype(jnp.float32)) * (HEAD_DIM ** -0.5)
    s = s + bias[None, :, None]
    if mask is not None:
        s = jnp.where(mask, s, NEG_INF)
    m = jnp.max(s, axis=-1)
    p = jnp.exp(s - m[..., None])
    return m, jnp.sum(p, axis=-1), jnp.einsum('bhmqk,bhkv->bhmqv', p, vh.astype(jnp.float32))


def _merge(a, b):
    m1, l1, o1 = a
    m2, l2, o2 = b
    m = jnp.maximum(m1, m2)
    s1 = jnp.exp(m1 - m)
    s2 = jnp.exp(m2 - m)
    return m, l1 * s1 + l2 * s2, o1 * s1[..., None] + o2 * s2[..., None]


def _diff_combine(l, acc, lam):
    o = acc / l[..., None]
    return o[:, :, 0] - lam * o[:, :, 1]


def _attend_prompt(qh, k_rows, v_rows, lam, rel_table):
    b, s = k_rows.shape[:2]
    kh = _heads_qk(k_rows)
    vh = v_rows.transpose(0, 2, 1, 3)
    nqb = s // Q_BLOCK
    q_blocks = jnp.moveaxis(qh.reshape(b, N_HEADS, 2, nqb, Q_BLOCK, HEAD_DIM), 3, 0)
    kpos = jnp.arange(s)

    def one_block(args):
        i, qb = args
        qpos = i * Q_BLOCK + jnp.arange(Q_BLOCK)
        _, l, acc = _attn_partial(qb, kh, vh, _rel_bias(rel_table, qpos, kpos), kpos[None, :] <= qpos[:, None])
        return _diff_combine(l, acc, lam)

    o = lax.map(one_block, (jnp.arange(nqb), q_blocks))
    return jnp.moveaxis(o, 0, 2).reshape(b, N_HEADS, s, 2 * HEAD_DIM)


def _attend_sample(qh, k_rows, v_rows, lam, rel_table, cache_k, cache_v, page_table, layer):
    b, s = k_rows.shape[:2]
    n_pages = page_table.shape[1]
    qpos = PAST_LEN + jnp.arange(s)
    carry = _attn_partial(qh, _heads_qk(k_rows), v_rows.transpose(0, 2, 1, 3),
                          _rel_bias(rel_table, qpos, qpos), qpos[None, :] <= qpos[:, None])

    def page_step(c, xs):
        p, pids = xs
        kp = cache_k[layer, pids]
        vp = cache_v[layer, pids]
        kpos = p * PAGE_SIZE + jnp.arange(PAGE_SIZE)
        part = _attn_partial(qh, _heads_qk(kp), vp.transpose(0, 2, 1, 3), _rel_bias(rel_table, qpos, kpos), None)
        return _merge(c, part), None

    (_, l, acc), _ = lax.scan(page_step, carry, (jnp.arange(n_pages), page_table.T))
    return _diff_combine(l, acc, lam)


def _lin_comb(c1, c2):
    a1, b1 = c1
    a2, b2 = c2
    return a1 * a2, a2 * b1 + b2


def _rglru(xr, gate, conv_prev, h_prev, conv_w, conv_b, w_rg, b_rg, w_ig, b_ig, lru_lambda):
    b, s, r = xr.shape
    xc = jnp.concatenate([conv_prev.astype(xr.dtype), xr], axis=1)
    conv = conv_b
    for j in range(CONV_W):
        conv = conv + xc[:, j:j + s] * conv_w[j]
    conv_new = xc[:, s:]
    cf = conv.astype(jnp.float32)
    xb = cf.reshape(b, s, LRU_BLOCKS, LRU_BLOCK_W)
    rg = jax.nn.sigmoid(jnp.einsum('bsnc,ncd->bsnd', xb, w_rg.astype(jnp.float32)) + b_rg.astype(jnp.float32)).reshape(b, s, r)
    ig = jax.nn.sigmoid(jnp.einsum('bsnc,ncd->bsnd', xb, w_ig.astype(jnp.float32)) + b_ig.astype(jnp.float32)).reshape(b, s, r)
    log_a = -LRU_C * rg * jax.nn.softplus(-lru_lambda.astype(jnp.float32))
    a = jnp.exp(log_a)
    bx = jnp.sqrt(-jnp.expm1(2.0 * log_a)) * ig * cf
    bx = bx.at[:, 0].add(a[:, 0] * h_prev.astype(jnp.float32))
    _, h = lax.associative_scan(_lin_comb, (a, bx), axis=1)
    y = (h * jax.nn.gelu(gate.astype(jnp.float32))).astype(xr.dtype)
    return y, conv_new, h[:, -1].astype(h_prev.dtype)


def _mixer(xn, conv_prev, h_prev, attend, lam, lam_init, w_in, subln_g, conv_w, conv_b,
           w_rg, b_rg, w_ig, b_ig, lru_lambda, w_out):
    b, s, _ = xn.shape
    proj = xn @ w_in
    q, k, v, xr, gate = jnp.split(proj, [ATTN_WIDTH, 2 * ATTN_WIDTH, 3 * ATTN_WIDTH,
                                         3 * ATTN_WIDTH + LRU_WIDTH], axis=-1)
    k_rows = k.reshape(b, s, N_HEADS, 2 * HEAD_DIM)
    v_rows = v.reshape(b, s, N_HEADS, 2 * HEAD_DIM)
    o = attend(_heads_qk(q.reshape(b, s, N_HEADS, 2 * HEAD_DIM)), k_rows, v_rows, lam)
    o = _rmsnorm(o, subln_g) * (1.0 - lam_init)
    attn_out = o.transpose(0, 2, 1, 3).reshape(b, s, ATTN_WIDTH).astype(xn.dtype)
    lru_out, conv_new, h_new = _rglru(xr, gate, conv_prev, h_prev, conv_w, conv_b, w_rg, b_rg, w_ig, b_ig, lru_lambda)
    y = jnp.concatenate([attn_out, lru_out], axis=-1) @ w_out
    return y, k_rows, v_rows, conv_new, h_new


def _moe(x, layer, router_w, router_b, w_gate_up, b_gate_up, w_down, b_down):
    shape = x.shape
    xt = x.reshape(-1, D_MODEL)
    n_tok = xt.shape[0]
    logits = (xt @ router_w[layer] + router_b[layer]).astype(jnp.float32)
    top_v, top_i = lax.top_k(logits, TOP_K)
    gates = jax.nn.softmax(top_v, axis=-1).reshape(-1)
    flat_e = top_i.reshape(-1)
    n_assign = n_tok * TOP_K
    order = jnp.argsort(flat_e)
    sorted_e = flat_e[order]
    counts = jnp.zeros((N_EXPERTS,), jnp.int32).at[flat_e].add(1)
    padded = (counts + MOE_BLOCK - 1) // MOE_BLOCK * MOE_BLOCK
    pad_end = jnp.cumsum(padded)
    pad_start = pad_end - padded
    start = jnp.cumsum(counts) - counts
    dest = pad_start[sorted_e] + (jnp.arange(n_assign) - start[sorted_e])
    n_blocks = -(-n_assign // MOE_BLOCK) + N_EXPERTS
    n_slots = n_blocks * MOE_BLOCK
    slot_tok = jnp.full((n_slots,), n_tok, jnp.int32).at[dest].set((order // TOP_K).astype(jnp.int32))
    slot_gate = jnp.zeros((n_slots,), jnp.float32).at[dest].set(gates[order])
    block_e = jnp.minimum(jnp.searchsorted(pad_end, jnp.arange(n_blocks) * MOE_BLOCK, side='right'), N_EXPERTS - 1)
    x_pad = jnp.concatenate([xt, jnp.zeros((1, D_MODEL), xt.dtype)], axis=0)

    def expert_block(out, xs):
        e, tok, g = xs
        gu = x_pad[tok] @ w_gate_up[layer, e] + b_gate_up[layer, e]
        glu = jnp.minimum(gu[:, :D_FF], SWIGLU_LIMIT)
        lin = jnp.clip(gu[:, D_FF:], -SWIGLU_LIMIT, SWIGLU_LIMIT)
        act = glu * jax.nn.sigmoid(SWIGLU_ALPHA * glu) * (lin + 1.0)
        y = act @ w_down[layer, e] + b_down[layer, e]
        return out.at[tok].add((y * g[:, None].astype(y.dtype)).astype(out.dtype)), None

    out0 = jnp.zeros((n_tok + 1, D_MODEL), xt.dtype)
    out, _ = lax.scan(expert_block, out0, (block_e, slot_tok.reshape(n_blocks, MOE_BLOCK),
                                           slot_gate.reshape(n_blocks, MOE_BLOCK)))
    return out[:n_tok].reshape(shape)


def setup_inputs(seed: int = 0) -> dict:
    key = jax.random.key(seed)
    ks = jax.random.split(key, 32)
    f = jnp.float32
    n_pages = PAST_LEN // PAGE_SIZE
    n_pool = (DEC_BATCH * n_pages * 5) // 4

    def nrm(k, shape, scale):
        return jax.random.normal(k, shape, f) * scale

    u = jax.random.uniform(ks[0], (DEPTH, LRU_WIDTH), f, 0.9, 0.999)
    sroot = u ** (1.0 / LRU_C)
    lru_lambda = jnp.log(sroot) - jnp.log1p(-sroot)
    perm = jax.random.permutation(ks[1], n_pool)
    page_table = perm[:DEC_BATCH * n_pages].reshape(DEC_BATCH, n_pages).astype(jnp.int32)
    kv_shape = (DEPTH, n_pool, PAGE_SIZE, N_HEADS, 2 * HEAD_DIM)
    return {
        'x_prompt': nrm(ks[2], (BATCH, SEQ, D_MODEL), 1.0),
        'x_sample': nrm(ks[3], (DEC_BATCH, DEC_SEQ, D_MODEL), 1.0),
        'cache_k': nrm(ks[4], kv_shape, 1.0),
        'cache_v': nrm(ks[5], kv_shape, 1.0),
        'state_conv': nrm(ks[6], (DEPTH, DEC_BATCH, CONV_W - 1, LRU_WIDTH), 1.0),
        'state_h': nrm(ks[7], (DEPTH, DEC_BATCH, LRU_WIDTH), 0.5),
        'page_table': page_table,
        'rel_bias': nrm(ks[8], (REL_BUCKETS, N_HEADS), 0.5),
        'norm_mix_g': 1.0 + nrm(ks[9], (DEPTH, D_MODEL), 0.02),
        'w_in': nrm(ks[10], (DEPTH, D_MODEL, PROJ_WIDTH), D_MODEL ** -0.5),
        'lam_q1': nrm(ks[11], (DEPTH, HEAD_DIM), 0.1),
        'lam_k1': nrm(ks[12], (DEPTH, HEAD_DIM), 0.1),
        'lam_q2': nrm(ks[13], (DEPTH, HEAD_DIM), 0.1),
        'lam_k2': nrm(ks[14], (DEPTH, HEAD_DIM), 0.1),
        'subln_g': 1.0 + nrm(ks[15], (DEPTH, 2 * HEAD_DIM), 0.02),
        'conv_w': nrm(ks[16], (DEPTH, CONV_W, LRU_WIDTH), CONV_W ** -0.5),
        'conv_b': nrm(ks[17], (DEPTH, LRU_WIDTH), 0.02),
        'w_rg': nrm(ks[18], (DEPTH, LRU_BLOCKS, LRU_BLOCK_W, LRU_BLOCK_W), LRU_BLOCK_W ** -0.5),
        'b_rg': nrm(ks[19], (DEPTH, LRU_BLOCKS, LRU_BLOCK_W), 0.02),
        'w_ig': nrm(ks[20], (DEPTH, LRU_BLOCKS, LRU_BLOCK_W, LRU_BLOCK_W), LRU_BLOCK_W ** -0.5),
        'b_ig': nrm(ks[21], (DEPTH, LRU_BLOCKS, LRU_BLOCK_W), 0.02),
        'lru_lambda': lru_lambda,
        'w_out': nrm(ks[22], (DEPTH, D_MODEL, D_MODEL), D_MODEL ** -0.5),
        'norm_ffn_g': 1.0 + nrm(ks[23], (DEPTH, D_MODEL), 0.02),
        'router_w': nrm(ks[24], (DEPTH, D_MODEL, N_EXPERTS), D_MODEL ** -0.5),
        'router_b': nrm(ks[25], (DEPTH, N_EXPERTS), 0.01),
        'w_gate_up': nrm(ks[26], (DEPTH, N_EXPERTS, D_MODEL, 2 * D_FF), D_MODEL ** -0.5),
        'b_gate_up': nrm(ks[27], (DEPTH, N_EXPERTS, 2 * D_FF), 0.01),
        'w_down': nrm(ks[28], (DEPTH, N_EXPERTS, D_FF, D_MODEL), D_FF ** -0.5),
        'b_down': nrm(ks[29], (DEPTH, N_EXPERTS, D_MODEL), 0.01),
        'norm_final_g': 1.0 + nrm(ks[30], (D_MODEL,), 0.02),
    }


def reference(x_prompt, x_sample, cache_k, cache_v, state_conv, state_h, page_table, rel_bias,
              norm_mix_g, w_in, lam_q1, lam_k1, lam_q2, lam_k2, subln_g, conv_w, conv_b,
              w_rg, b_rg, w_ig, b_ig, lru_lambda, w_out, norm_ffn_g, router_w, router_b,
              w_gate_up, b_gate_up, w_down, b_down, norm_final_g):
    f = jnp.float32
    xp = x_prompt
    xs = x_sample
    bp = x_prompt.shape[0]
    kp_l, vp_l, cp_l, hp_l = [], [], [], []
    ks_l, vs_l, cs_l, hs_l = [], [], [], []
    for l in range(DEPTH):
        lam_init = _lambda_init(l)
        lam = (jnp.exp(jnp.sum(lam_q1[l].astype(f) * lam_k1[l].astype(f)))
               - jnp.exp(jnp.sum(lam_q2[l].astype(f) * lam_k2[l].astype(f))) + lam_init)
        attend_p = functools.partial(_attend_prompt, rel_table=rel_bias)
        attend_s = functools.partial(_attend_sample, rel_table=rel_bias, cache_k=cache_k,
                                     cache_v=cache_v, page_table=page_table, layer=l)
        conv0 = jnp.zeros((bp, CONV_W - 1, LRU_WIDTH), x_prompt.dtype)
        h0 = jnp.zeros((bp, LRU_WIDTH), x_prompt.dtype)
        yp, kp, vp, cp, hp = _mixer(_rmsnorm(xp, norm_mix_g[l]), conv0, h0, attend_p, lam, lam_init,
                                    w_in[l], subln_g[l], conv_w[l], conv_b[l], w_rg[l], b_rg[l],
                                    w_ig[l], b_ig[l], lru_lambda[l], w_out[l])
        xp = xp + yp
        xp = xp + _moe(_rmsnorm(xp, norm_ffn_g[l]), l, router_w, router_b, w_gate_up, b_gate_up, w_down, b_down)
        ys, ksm, vsm, csm, hsm = _mixer(_rmsnorm(xs, norm_mix_g[l]), state_conv[l], state_h[l], attend_s, lam, lam_init,
                                        w_in[l], subln_g[l], conv_w[l], conv_b[l], w_rg[l], b_rg[l],
                                        w_ig[l], b_ig[l], lru_lambda[l], w_out[l])
        xs = xs + ys
        xs = xs + _moe(_rmsnorm(xs, norm_ffn_g[l]), l, router_w, router_b, w_gate_up, b_gate_up, w_down, b_down)
        kp_l.append(kp); vp_l.append(vp); cp_l.append(cp); hp_l.append(hp)
        ks_l.append(ksm); vs_l.append(vsm); cs_l.append(csm); hs_l.append(hsm)
    y_prompt = _rmsnorm(xp, norm_final_g)
    y_sample = _rmsnorm(xs, norm_final_g)
    return (y_prompt, y_sample,
            jnp.stack(kp_l), jnp.stack(vp_l), jnp.stack(cp_l), jnp.stack(hp_l),
            jnp.stack(ks_l), jnp.stack(vs_l), jnp.stack(cs_l), jnp.stack(hs_l))
```

```python
import functools
import math

import jax
import jax.numpy as jnp
from jax import lax
from jax.experimental import pallas as pl
from jax.experimental.pallas import tpu as pltpu

F32 = jnp.float32
BF16 = jnp.bfloat16

D_MODEL = 2048
ATTN_WIDTH = D_MODEL // 2
LRU_WIDTH = D_MODEL - ATTN_WIDTH
HEAD_DIM = 64
HEAD_W = 2 * HEAD_DIM
N_HEADS = ATTN_WIDTH // HEAD_W
LRU_BLOCKS = 8
LRU_BLOCK_W = LRU_WIDTH // LRU_BLOCKS
CONV_W = 4
LRU_C = 8.0
REL_BUCKETS = 32
REL_EXACT = REL_BUCKETS // 2
REL_MAX_DIST = 128
N_EXPERTS = 32
TOP_K = 4
D_FF = D_MODEL
SWIGLU_LIMIT = 7.0
SWIGLU_ALPHA = 1.702
EPS = 1e-5
NEG_INF = -1e30
N_PROJ = 5
PROJ_W = 1024

VMEM_LIMIT = 56 * 1024 * 1024
SUBLANES = 8

ATTN_TILE = 256
LRU_TILE = 256
ROW_TILE = 512
MOE_TILE = 512
MOE_FF_TILE = 512
PAGES_PER_STEP = 4


def _cparams(*sem):
    return pltpu.CompilerParams(dimension_semantics=sem, vmem_limit_bytes=VMEM_LIMIT)


def _rms(x, g):
    return x * lax.rsqrt(jnp.mean(x * x, axis=-1, keepdims=True) + EPS) * g


def _inproj_kernel(x_ref, g_ref, w_ref, q_ref, k_ref, v_ref, xr_ref, gate_ref, xn_sc):
    j = pl.program_id(1)

    @pl.when(j == 0)
    def _():
        xn_sc[...] = _rms(x_ref[...], g_ref[...]).astype(BF16)

    acc = jnp.dot(xn_sc[...], w_ref[...], preferred_element_type=F32)

    @pl.when(j == 0)
    def _():
        q_ref[...] = (acc * (HEAD_DIM ** -0.5)).astype(BF16)

    for jj, ref in ((1, k_ref), (2, v_ref), (3, xr_ref), (4, gate_ref)):
        @pl.when(j == jj)
        def _(ref=ref):
            ref[...] = acc


def _inproj(x2, g, w_bf, tm):
    n = x2.shape[0]
    row = lambda i, j: (i, 0)
    outs = [jax.ShapeDtypeStruct((n, PROJ_W), BF16)] + [jax.ShapeDtypeStruct((n, PROJ_W), F32)] * 4
    return pl.pallas_call(
        _inproj_kernel,
        grid=(n // tm, N_PROJ),
        in_specs=[pl.BlockSpec((tm, D_MODEL), row),
                  pl.BlockSpec((1, D_MODEL), lambda i, j: (0, 0)),
                  pl.BlockSpec((D_MODEL, PROJ_W), lambda i, j: (0, j))],
        out_specs=[pl.BlockSpec((tm, PROJ_W), row)] * 5,
        out_shape=outs,
        scratch_shapes=[pltpu.VMEM((tm, D_MODEL), BF16)],
        compiler_params=_cparams("parallel", "arbitrary"),
        name="inproj",
    )(x2, g, w_bf)


def _split_maps(q):
    lane = lax.broadcasted_iota(jnp.int32, q.shape, 1)
    zero = jnp.zeros_like(q)
    return jnp.concatenate([jnp.where(lane < HEAD_DIM, q, zero),
                            jnp.where(lane >= HEAD_DIM, q, zero)], axis=0)


def _softmax_block(s, vj, m_sc, l_sc, acc_sc):
    m_prev = m_sc[...]
    m_new = jnp.maximum(m_prev, jnp.max(s, axis=-1, keepdims=True))
    alpha = jnp.exp(m_prev - m_new)
    p = jnp.exp(s - m_new)
    l_sc[...] = alpha * l_sc[...] + jnp.sum(p, axis=-1, keepdims=True)
    acc_sc[...] = alpha * acc_sc[...] + jnp.dot(p.astype(BF16), vj, preferred_element_type=F32)
    m_sc[...] = m_new


def _diff_finish(l_sc, acc_sc, lam, g, out_scale):
    o = acc_sc[...] / l_sc[...]
    r = o.shape[0] // 2
    d = o[:r] - lam * o[r:]
    return _rms(d, g) * out_scale


def _qk(q2, kj):
    return lax.dot_general(q2, kj, (((1,), (1,)), ((), ())), preferred_element_type=F32)


def _prompt_attn_kernel(far_ref, lam_ref, q_ref, k_ref, v_ref, t0_ref, t1_ref, g_ref, o_ref,
                        kb_sc, vb_sc, m_sc, l_sc, acc_sc, *, tile, out_scale):
    h = pl.program_id(1)
    i = pl.program_id(2)

    @pl.when(i == 0)
    def _():
        kb_sc[...] = k_ref[...].astype(BF16)
        vb_sc[...] = v_ref[...].astype(BF16)

    q2 = _split_maps(q_ref[...])
    m_sc[...] = jnp.full(m_sc.shape, NEG_INF, F32)
    l_sc[...] = jnp.zeros(l_sc.shape, F32)
    acc_sc[...] = jnp.zeros(acc_sc.shape, F32)

    def block(j, bias):
        start = pl.multiple_of(j * tile, tile)
        kj = kb_sc[pl.ds(start, tile), :]
        vj = vb_sc[pl.ds(start, tile), :]
        s = _qk(q2, kj).reshape(2, tile, tile) + bias
        _softmax_block(s.reshape(2 * tile, tile), vj, m_sc, l_sc, acc_sc)

    far = far_ref[h]

    def far_body(j, c):
        block(j, far)
        return c

    lax.fori_loop(0, i - 1, far_body, 0)

    @pl.when(i >= 1)
    def _():
        block(i - 1, t1_ref[0][None])

    block(i, t0_ref[0][None])
    o_ref[...] = _diff_finish(l_sc, acc_sc, lam_ref[0], g_ref[...], out_scale).astype(o_ref.dtype)


def _prompt_attention(q, k, v, t0, t1, far, lam, subln_g, batch, seq, out_scale):
    tile = ATTN_TILE
    nq = seq // tile
    kern = functools.partial(_prompt_attn_kernel, tile=tile, out_scale=out_scale)
    smem = pl.BlockSpec(memory_space=pltpu.SMEM)
    qmap = lambda b, h, i: (b * nq + i, h)
    kvmap = lambda b, h, i: (b, h)
    tmap = lambda b, h, i: (h, 0, 0)
    return pl.pallas_call(
        kern,
        grid=(batch, N_HEADS, nq),
        in_specs=[smem, smem,
                  pl.BlockSpec((tile, HEAD_W), qmap),
                  pl.BlockSpec((seq, HEAD_W), kvmap),
                  pl.BlockSpec((seq, HEAD_W), kvmap),
                  pl.BlockSpec((1, tile, tile), tmap),
                  pl.BlockSpec((1, tile, tile), tmap),
                  pl.BlockSpec((1, HEAD_W), lambda b, h, i: (0, 0))],
        out_specs=pl.BlockSpec((tile, HEAD_W), qmap),
        out_shape=jax.ShapeDtypeStruct((batch * seq, ATTN_WIDTH), BF16),
        scratch_shapes=[pltpu.VMEM((seq, HEAD_W), BF16), pltpu.VMEM((seq, HEAD_W), BF16),
                        pltpu.VMEM((2 * tile, 1), F32), pltpu.VMEM((2 * tile, 1), F32),
                        pltpu.VMEM((2 * tile, HEAD_W), F32)],
        compiler_params=_cparams("parallel", "parallel", "arbitrary"),
        name="prompt_attn",
    )(far, lam, q, k, v, t0, t1, subln_g)


def _sample_attn_kernel(pt_ref, lam_ref, q_ref, *refs, n_pp, out_scale):
    k_refs = refs[:n_pp]
    v_refs = refs[n_pp:2 * n_pp]
    tab_ref, ks_ref, vs_ref, tabs_ref, g_ref, o_ref, m_sc, l_sc, acc_sc = refs[2 * n_pp:]
    s_idx = pl.program_id(1)
    cols = k_refs[0].shape[1]

    @pl.when(s_idx == 0)
    def _():
        m_sc[...] = jnp.full(m_sc.shape, NEG_INF, F32)
        l_sc[...] = jnp.zeros(l_sc.shape, F32)
        acc_sc[...] = jnp.zeros(acc_sc.shape, F32)

    q2 = _split_maps(q_ref[0])
    for pp in range(n_pp):
        kj = k_refs[pp][0].astype(BF16)
        vj = v_refs[pp][0].astype(BF16)
        s = _qk(q2, kj) + tab_ref[0, :, pp * cols:(pp + 1) * cols]
        _softmax_block(s, vj, m_sc, l_sc, acc_sc)

    @pl.when(s_idx == pl.num_programs(1) - 1)
    def _():
        s = _qk(q2, ks_ref[0].astype(BF16)) + tabs_ref[...]
        _softmax_block(s, vs_ref[0].astype(BF16), m_sc, l_sc, acc_sc)
        o_ref[0] = _diff_finish(l_sc, acc_sc, lam_ref[0], g_ref[...], out_scale).astype(o_ref.dtype)


def _sample_attention(q3, cache_k3, cache_v3, page_table, tab, k_self, v_self, tab_self, lam, subln_g,
                      out_scale):
    nb, rows, _ = q3.shape
    n_pages = page_table.shape[1]
    n_pp = PAGES_PER_STEP
    n_steps = n_pages // n_pp
    cols = cache_k3.shape[1]
    kern = functools.partial(_sample_attn_kernel, n_pp=n_pp, out_scale=out_scale)

    def page_map(pp):
        return lambda b, s, pt: (pt[b, s * n_pp + pp], 0, 0)

    page_specs = [pl.BlockSpec((1, cols, HEAD_W), page_map(pp)) for pp in range(n_pp)]
    bmap = lambda b, s, pt: (b, 0, 0)
    grid_spec = pltpu.PrefetchScalarGridSpec(
        num_scalar_prefetch=1,
        grid=(nb, n_steps),
        in_specs=[pl.BlockSpec(memory_space=pltpu.SMEM),
                  pl.BlockSpec((1, rows, HEAD_W), bmap)]
                 + page_specs + page_specs
                 + [pl.BlockSpec((1, 2 * rows, n_pp * cols),
                                 lambda b, s, pt: (jnp.where(s == n_steps - 1, 1, 0), 0, 0)),
                    pl.BlockSpec((1, HEAD_W, HEAD_W), bmap),
                    pl.BlockSpec((1, HEAD_W, HEAD_W), bmap),
                    pl.BlockSpec((2 * rows, HEAD_W), lambda b, s, pt: (0, 0)),
                    pl.BlockSpec((1, HEAD_W), lambda b, s, pt: (0, 0))],
        out_specs=pl.BlockSpec((1, rows, HEAD_W), bmap),
        scratch_shapes=[pltpu.VMEM((2 * rows, 1), F32), pltpu.VMEM((2 * rows, 1), F32),
                        pltpu.VMEM((2 * rows, HEAD_W), F32)],
    )
    return pl.pallas_call(
        kern,
        grid_spec=grid_spec,
        out_shape=jax.ShapeDtypeStruct((nb, rows, HEAD_W), BF16),
        compiler_params=_cparams("parallel", "arbitrary"),
        name="sample_attn",
    )(page_table, lam, q3, *([cache_k3] * n_pp), *([cache_v3] * n_pp), tab, k_self, v_self, tab_self,
      subln_g)


def _softplus(x):
    return jnp.maximum(x, 0.0) + jnp.log1p(jnp.exp(-jnp.abs(x)))


def _gelu_tanh(x):
    return 0.5 * x * (1.0 + jnp.tanh(math.sqrt(2.0 / math.pi) * (x + 0.044715 * x * x * x)))


def _block_gate(cb, w_ref, b):
    parts = [jnp.dot(cb[:, n * LRU_BLOCK_W:(n + 1) * LRU_BLOCK_W], w_ref[n], preferred_element_type=F32)
             for n in range(LRU_BLOCKS)]
    return jax.nn.sigmoid(jnp.concatenate(parts, axis=-1) + b)


def _lru_coeffs(conv, wrg_ref, brg, wig_ref, big, lam):
    cb = conv.astype(BF16)
    rg = _block_gate(cb, wrg_ref, brg)
    ig = _block_gate(cb, wig_ref, big)
    log_a = -LRU_C * rg * _softplus(-lam)
    a = jnp.exp(log_a)
    bx = jnp.sqrt(jnp.tanh(-log_a) * (1.0 + a * a)) * ig * conv
    return a, bx


def _prompt_lru_kernel(xr_ref, gate_ref, cw_ref, cb_ref, wrg_ref, brg_ref, wig_ref, big_ref, lam_ref,
                       y_ref, conv_ref, h_ref, xc_sc, h_sc, *, tile):
    t = pl.program_id(1)
    pad = SUBLANES

    @pl.when(t == 0)
    def _():
        xc_sc[0:pad, :] = jnp.zeros((pad, LRU_WIDTH), F32)
        h_sc[...] = jnp.zeros(h_sc.shape, F32)

    xr = xr_ref[...]
    xc_sc[pad:pad + tile, :] = xr
    conv = cb_ref[...]
    for j in range(CONV_W):
        off = pad - (CONV_W - 1) + j
        conv = conv + xc_sc[off:off + tile, :] * cw_ref[j:j + 1, :]
    xc_sc[0:pad, :] = xr[tile - pad:, :]

    a, bx = _lru_coeffs(conv, wrg_ref, brg_ref[...], wig_ref, big_ref[...], lam_ref[...])

    row = lax.broadcasted_iota(jnp.int32, a.shape, 0)
    d = 1
    while d < tile:
        keep = row >= d
        a_sh = pltpu.roll(a, d, 0)
        b_sh = pltpu.roll(bx, d, 0)
        bx = jnp.where(keep, a * b_sh + bx, bx)
        a = jnp.where(keep, a * a_sh, a)
        d *= 2
    h = bx + a * h_sc[...]
    h_sc[...] = h[tile - 1:tile, :]
    y_ref[...] = (h * _gelu_tanh(gate_ref[...])).astype(y_ref.dtype)

    @pl.when(t == pl.num_programs(1) - 1)
    def _():
        conv_ref[0] = xr[tile - (CONV_W - 1):, :]
        h_ref[0] = h[tile - 1:tile, :]


def _prompt_lru(xr, gate, conv_w, conv_b, w_rg, b_rg, w_ig, b_ig, lru_lambda, batch, seq):
    tile = LRU_TILE
    nt = seq // tile
    kern = functools.partial(_prompt_lru_kernel, tile=tile)
    rmap = lambda b, t: (b * nt + t, 0)
    full2 = lambda b, t: (0, 0)
    full3 = lambda b, t: (0, 0, 0)
    bmap = lambda b, t: (b, 0, 0)
    vec = pl.BlockSpec((1, LRU_WIDTH), full2)
    wspec = pl.BlockSpec((LRU_BLOCKS, LRU_BLOCK_W, LRU_BLOCK_W), full3)
    return pl.pallas_call(
        kern,
        grid=(batch, nt),
        in_specs=[pl.BlockSpec((tile, LRU_WIDTH), rmap), pl.BlockSpec((tile, LRU_WIDTH), rmap),
                  pl.BlockSpec((CONV_W, LRU_WIDTH), full2), vec, wspec, vec, wspec, vec, vec],
        out_specs=[pl.BlockSpec((tile, LRU_WIDTH), rmap),
                   pl.BlockSpec((1, CONV_W - 1, LRU_WIDTH), bmap),
                   pl.BlockSpec((1, 1, LRU_WIDTH), bmap)],
        out_shape=[jax.ShapeDtypeStruct((batch * seq, LRU_WIDTH), BF16),
                   jax.ShapeDtypeStruct((batch, CONV_W - 1, LRU_WIDTH), F32),
                   jax.ShapeDtypeStruct((batch, 1, LRU_WIDTH), F32)],
        scratch_shapes=[pltpu.VMEM((tile + SUBLANES, LRU_WIDTH), F32), pltpu.VMEM((1, LRU_WIDTH), F32)],
        compiler_params=_cparams("parallel", "arbitrary"),
        name="prompt_lru",
    )(xr, gate, conv_w, conv_b, w_rg, b_rg, w_ig, b_ig, lru_lambda)


def _sample_lru_kernel(xr_ref, gate_ref, cprev_ref, hprev_ref, cw_ref, cb_ref, wrg_ref, brg_ref,
                       wig_ref, big_ref, lam_ref, y_ref, conv_ref, h_ref, *, steps):
    xcat = [cprev_ref[j] for j in range(CONV_W - 1)] + [xr_ref[t] for t in range(steps)]
    h = hprev_ref[...]
    for t in range(steps):
        conv = cb_ref[...]
        for j in range(CONV_W):
            conv = conv + xcat[t + j] * cw_ref[j:j + 1, :]
        a, bx = _lru_coeffs(conv, wrg_ref, brg_ref[...], wig_ref, big_ref[...], lam_ref[...])
        h = a * h + bx
        y_ref[t] = (h * _gelu_tanh(gate_ref[t])).astype(y_ref.dtype)
    for j in range(CONV_W - 1):
        conv_ref[j] = xcat[steps + j]
    h_ref[...] = h


def _sample_lru(xr_t, gate_t, cprev_t, hprev, conv_w, conv_b, w_rg, b_rg, w_ig, b_ig, lru_lambda):
    steps, nb, _ = xr_t.shape
    kern = functools.partial(_sample_lru_kernel, steps=steps)
    return pl.pallas_call(
        kern,
        out_shape=[jax.ShapeDtypeStruct((steps, nb, LRU_WIDTH), BF16),
                   jax.ShapeDtypeStruct((CONV_W - 1, nb, LRU_WIDTH), F32),
                   jax.ShapeDtypeStruct((nb, LRU_WIDTH), F32)],
        compiler_params=pltpu.CompilerParams(vmem_limit_bytes=VMEM_LIMIT),
        name="sample_lru",
    )(xr_t, gate_t, cprev_t, hprev, conv_w, conv_b, w_rg, b_rg, w_ig, b_ig, lru_lambda)


def _outproj_kernel(a_ref, l_ref, w_ref, x_ref, g_ref, rw_ref, rb_ref, xmid_ref, xn_ref, logit_ref):
    y = jnp.dot(a_ref[...], w_ref[0:ATTN_WIDTH, :], preferred_element_type=F32)
    y = y + jnp.dot(l_ref[...], w_ref[ATTN_WIDTH:, :], preferred_element_type=F32)
    xm = x_ref[...] + y
    xmid_ref[...] = xm
    xn = _rms(xm, g_ref[...])
    xn_ref[...] = xn.astype(BF16)
    logit_ref[...] = jnp.dot(xn, rw_ref[...], preferred_element_type=F32,
                             precision=lax.Precision.HIGHEST) + rb_ref[...]


def _outproj(attn, lru, w_out_bf, x2, g, router_w, router_b, tm):
    n = x2.shape[0]
    row = lambda i: (i, 0)
    full = lambda i: (0, 0)
    return pl.pallas_call(
        _outproj_kernel,
        grid=(n // tm,),
        in_specs=[pl.BlockSpec((tm, ATTN_WIDTH), row), pl.BlockSpec((tm, LRU_WIDTH), row),
                  pl.BlockSpec((D_MODEL, D_MODEL), full), pl.BlockSpec((tm, D_MODEL), row),
                  pl.BlockSpec((1, D_MODEL), full), pl.BlockSpec((D_MODEL, N_EXPERTS), full),
                  pl.BlockSpec((1, N_EXPERTS), full)],
        out_specs=[pl.BlockSpec((tm, D_MODEL), row), pl.BlockSpec((tm, D_MODEL), row),
                   pl.BlockSpec((tm, N_EXPERTS), row)],
        out_shape=[jax.ShapeDtypeStruct((n, D_MODEL), F32), jax.ShapeDtypeStruct((n, D_MODEL), BF16),
                   jax.ShapeDtypeStruct((n, N_EXPERTS), F32)],
        compiler_params=_cparams("parallel"),
        name="outproj",
    )(attn, lru, w_out_bf, x2, g, router_w, router_b)


def _moe_kernel(te_ref, na_ref, x_ref, wg_ref, wu_ref, wd_ref, bg_ref, bu_ref, bd_ref, o_ref):
    i = pl.program_id(0)
    j = pl.program_id(1)

    @pl.when(i < na_ref[0])
    def _():
        x = x_ref[...]
        glu = jnp.dot(x, wg_ref[0].astype(BF16), preferred_element_type=F32) + bg_ref[0]
        lin = jnp.dot(x, wu_ref[0].astype(BF16), preferred_element_type=F32) + bu_ref[0]
        glu = jnp.minimum(glu, SWIGLU_LIMIT)
        lin = jnp.clip(lin, -SWIGLU_LIMIT, SWIGLU_LIMIT)
        act = glu * jax.nn.sigmoid(SWIGLU_ALPHA * glu) * (lin + 1.0)
        part = jnp.dot(act.astype(BF16), wd_ref[0].astype(BF16), preferred_element_type=F32)

        @pl.when(j == 0)
        def _():
            o_ref[...] = part + bd_ref[0]

        @pl.when(j > 0)
        def _():
            o_ref[...] += part


def _moe_ffn(x_sorted, tile_expert, n_active, w_gate_up, b_gate_up, w_down, b_down):
    n_slots = x_sorted.shape[0]
    tm, tf = MOE_TILE, MOE_FF_TILE
    n_tiles = n_slots // tm
    n_f = D_FF // tf
    last_f = n_f - 1

    def fidx(i, j, na):
        return jnp.where(i < na[0], j, last_f)

    def rowidx(i, na):
        return jnp.minimum(i, na[0] - 1)

    grid_spec = pltpu.PrefetchScalarGridSpec(
        num_scalar_prefetch=2,
        grid=(n_tiles, n_f),
        in_specs=[pl.BlockSpec((tm, D_MODEL), lambda i, j, te, na: (rowidx(i, na), 0)),
                  pl.BlockSpec((1, D_MODEL, tf), lambda i, j, te, na: (te[i], 0, fidx(i, j, na))),
                  pl.BlockSpec((1, D_MODEL, tf), lambda i, j, te, na: (te[i], 0, n_f + fidx(i, j, na))),
                  pl.BlockSpec((1, tf, D_MODEL), lambda i, j, te, na: (te[i], fidx(i, j, na), 0)),
                  pl.BlockSpec((1, 1, tf), lambda i, j, te, na: (te[i], 0, fidx(i, j, na))),
                  pl.BlockSpec((1, 1, tf), lambda i, j, te, na: (te[i], 0, n_f + fidx(i, j, na))),
                  pl.BlockSpec((1, 1, D_MODEL), lambda i, j, te, na: (te[i], 0, 0))],
        out_specs=pl.BlockSpec((tm, D_MODEL), lambda i, j, te, na: (rowidx(i, na), 0)),
    )
    bgu3 = b_gate_up.reshape(N_EXPERTS, 1, 2 * D_FF)
    bd3 = b_down.reshape(N_EXPERTS, 1, D_MODEL)
    return pl.pallas_call(
        _moe_kernel,
        grid_spec=grid_spec,
        out_shape=jax.ShapeDtypeStruct((n_slots, D_MODEL), F32),
        compiler_params=_cparams("arbitrary", "arbitrary"),
        name="moe_ffn",
    )(tile_expert, n_active, x_sorted, w_gate_up, w_gate_up, w_down, bgu3, bgu3, bd3)


def _final_kernel(x_ref, m_ref, g_ref, o_ref):
    o_ref[...] = _rms(x_ref[...] + m_ref[...], g_ref[...])


def _final(x_mid, moe_all, row_off, g, tm):
    n = x_mid.shape[0]
    boff = row_off // tm
    return pl.pallas_call(
        _final_kernel,
        grid=(n // tm,),
        in_specs=[pl.BlockSpec((tm, D_MODEL), lambda i: (i, 0)),
                  pl.BlockSpec((tm, D_MODEL), lambda i: (i + boff, 0)),
                  pl.BlockSpec((1, D_MODEL), lambda i: (0, 0))],
        out_specs=pl.BlockSpec((tm, D_MODEL), lambda i: (i, 0)),
        out_shape=jax.ShapeDtypeStruct((n, D_MODEL), F32),
        compiler_params=_cparams("parallel"),
        name="final_norm",
    )(x_mid, moe_all, g)


def _bias_of_distance(rel_table, n):
    nf = jnp.maximum(n, 1).astype(F32)
    large = REL_EXACT + (jnp.log(nf / REL_EXACT) / math.log(REL_MAX_DIST / REL_EXACT)
                         * (REL_BUCKETS - REL_EXACT)).astype(jnp.int32)
    bucket = jnp.where(n < REL_EXACT, n, jnp.minimum(large, REL_BUCKETS - 1))
    return rel_table[bucket].astype(F32)


def _prompt_bias_tables(rel_table, tile):
    d = jnp.arange(tile)[:, None] - jnp.arange(tile)[None, :]
    diag = jnp.where((d >= 0)[..., None], _bias_of_distance(rel_table, jnp.maximum(d, 0)), NEG_INF)
    near = _bias_of_distance(rel_table, d + tile)
    return jnp.transpose(diag, (2, 0, 1)), jnp.transpose(near, (2, 0, 1))


def _sample_bias_tables(rel_table, dec_seq, past_len, page_size):
    heads = jnp.arange(N_HEADS)
    same_head = heads[:, None] == heads[None, :]

    def table(dist, valid):
        b = _bias_of_distance(rel_table, jnp.maximum(dist, 0))
        b = jnp.transpose(b, (0, 2, 1))
        full = jnp.where(same_head[None, :, None, :] & valid[:, None, :, None],
                         b[:, :, :, None], NEG_INF)
        full = full.reshape(dec_seq * N_HEADS, -1)
        return jnp.concatenate([full, full], axis=0)

    qpos = past_len + jnp.arange(dec_seq)
    span = PAGES_PER_STEP * page_size
    always = jnp.ones((dec_seq, span), bool)
    k_far = jnp.arange(span)
    k_last = past_len - span + jnp.arange(span)
    tab_far = table(jnp.full((dec_seq, span), REL_MAX_DIST), always)
    tab_last = table(qpos[:, None] - k_last[None, :], always)
    del k_far
    n_self = HEAD_W // N_HEADS
    tk = jnp.arange(n_self)
    dist = jnp.arange(dec_seq)[:, None] - tk[None, :]
    tab_self = table(dist, (dist >= 0) & (tk[None, :] < dec_seq))
    return jnp.stack([tab_far, tab_last]), tab_self


def _route(logits, tm):
    n_tok = logits.shape[0]
    top_v, top_i = lax.top_k(logits, TOP_K)
    gates = jax.nn.softmax(top_v, axis=-1)
    flat_e = top_i.reshape(-1).astype(jnp.int32)
    n_assign = n_tok * TOP_K
    order = jnp.argsort(flat_e).astype(jnp.int32)
    sorted_e = flat_e[order]
    counts = jnp.sum((flat_e[:, None] == jnp.arange(N_EXPERTS)[None, :]).astype(jnp.int32), axis=0)
    padded = (counts + tm - 1) // tm * tm
    pad_end = jnp.cumsum(padded)
    pad_start = pad_end - padded
    start = jnp.cumsum(counts) - counts
    dest = pad_start[sorted_e] + (jnp.arange(n_assign, dtype=jnp.int32) - start[sorted_e])
    n_tiles = n_assign // tm + N_EXPERTS
    slot_tok = jnp.zeros((n_tiles * tm,), jnp.int32).at[dest].set(order // TOP_K)
    pos = jnp.zeros((n_assign,), jnp.int32).at[order].set(dest).reshape(n_tok, TOP_K)
    n_active = (pad_end[-1] // tm).astype(jnp.int32)
    tile_start = jnp.minimum(jnp.arange(n_tiles, dtype=jnp.int32), n_active - 1) * tm
    tile_expert = jnp.minimum(jnp.searchsorted(pad_end, tile_start, side='right'),
                              N_EXPERTS - 1).astype(jnp.int32)
    return gates, slot_tok, pos, tile_expert, n_active.reshape(1)


def kernel(x_prompt, x_sample, cache_k, cache_v, state_conv, state_h, page_table, rel_bias, norm_mix_g, w_in, lam_q1, lam_k1, lam_q2, lam_k2, subln_g, conv_w, conv_b, w_rg, b_rg, w_ig, b_ig, lru_lambda, w_out, norm_ffn_g, router_w, router_b, w_gate_up, b_gate_up, w_down, b_down, norm_final_g):
    depth = w_in.shape[0]
    assert depth == 1
    layer = 0
    bp, seq, _ = x_prompt.shape
    bs, dec_seq, _ = x_sample.shape
    n_pool, page_size = cache_k.shape[1], cache_k.shape[2]
    past_len = page_table.shape[1] * page_size
    n_p, n_s = bp * seq, bs * dec_seq

    lam_init = 0.8 - 0.6 * math.exp(-0.3 * layer)
    out_scale = 1.0 - lam_init
    lam = (jnp.exp(jnp.sum(lam_q1[layer] * lam_k1[layer])) - jnp.exp(jnp.sum(lam_q2[layer] * lam_k2[layer]))
           + lam_init).reshape(1).astype(F32)

    g_mix = norm_mix_g[layer].reshape(1, D_MODEL)
    g_ffn = norm_ffn_g[layer].reshape(1, D_MODEL)
    g_fin = norm_final_g.reshape(1, D_MODEL)
    g_sub = subln_g[layer].reshape(1, HEAD_W)
    w_in_bf = w_in[layer].astype(BF16)
    w_out_bf = w_out[layer].astype(BF16)
    lru_args = (conv_w[layer], conv_b[layer].reshape(1, LRU_WIDTH),
                w_rg[layer].astype(BF16), b_rg[layer].reshape(1, LRU_WIDTH),
                w_ig[layer].astype(BF16), b_ig[layer].reshape(1, LRU_WIDTH),
                lru_lambda[layer].reshape(1, LRU_WIDTH))
    rw = router_w[layer]
    rb = router_b[layer].reshape(1, N_EXPERTS)

    xp2 = x_prompt.reshape(n_p, D_MODEL)
    xs2 = x_sample.reshape(n_s, D_MODEL)

    q_p, k_p, v_p, xr_p, gate_p = _inproj(xp2, g_mix, w_in_bf, ROW_TILE)
    t0, t1 = _prompt_bias_tables(rel_bias, ATTN_TILE)
    far = rel_bias[REL_BUCKETS - 1].astype(F32)
    attn_p = _prompt_attention(q_p, k_p, v_p, t0, t1, far, lam, g_sub, bp, seq, out_scale)
    lru_p, conv_p, h_p = _prompt_lru(xr_p, gate_p, *lru_args, bp, seq)
    xmid_p, xn_p, logit_p = _outproj(attn_p, lru_p, w_out_bf, xp2, g_ffn, rw, rb, ROW_TILE)

    q_s, k_s, v_s, xr_s, gate_s = _inproj(xs2, g_mix, w_in_bf, n_s)
    tab, tab_self = _sample_bias_tables(rel_bias, dec_seq, past_len, page_size)
    rows = dec_seq * N_HEADS
    pad_rows = HEAD_W - rows

    def self_rows(a):
        a3 = a.reshape(bs, rows, HEAD_W)
        return jnp.concatenate([a3, jnp.zeros((bs, pad_rows, HEAD_W), a.dtype)], axis=1)

    attn_s = _sample_attention(
        q_s.reshape(bs, rows, HEAD_W),
        cache_k[layer].reshape(n_pool, page_size * N_HEADS, HEAD_W),
        cache_v[layer].reshape(n_pool, page_size * N_HEADS, HEAD_W),
        page_table, tab, self_rows(k_s), self_rows(v_s), tab_self, lam, g_sub, out_scale)
    attn_s = attn_s.reshape(n_s, ATTN_WIDTH)

    def tmajor(a):
        return jnp.transpose(a.reshape(bs, dec_seq, LRU_WIDTH), (1, 0, 2))

    lru_s_t, conv_s_t, h_s = _sample_lru(tmajor(xr_s), tmajor(gate_s),
                                         jnp.transpose(state_conv[layer], (1, 0, 2)), state_h[layer],
                                         *lru_args)
    lru_s = jnp.transpose(lru_s_t, (1, 0, 2)).reshape(n_s, LRU_WIDTH)
    conv_s = jnp.transpose(conv_s_t, (1, 0, 2))
    xmid_s, xn_s, logit_s = _outproj(attn_s, lru_s, w_out_bf, xs2, g_ffn, rw, rb, n_s)

    xn_all = jnp.concatenate([xn_p, xn_s], axis=0)
    logits = jnp.concatenate([logit_p, logit_s], axis=0)
    gates, slot_tok, pos, tile_expert, n_active = _route(logits, MOE_TILE)
    y_sorted = _moe_ffn(xn_all[slot_tok], tile_expert, n_active, w_gate_up[layer], b_gate_up[layer],
                        w_down[layer], b_down[layer])
    moe_all = jnp.sum(y_sorted[pos] * gates[..., None], axis=1)

    y_p = _final(xmid_p, moe_all, 0, g_fin, ROW_TILE)
    y_s = _final(xmid_s, moe_all, n_p, g_fin, n_s)

    kv_p = (depth, bp, seq, N_HEADS, HEAD_W)
    kv_s = (depth, bs, dec_seq, N_HEADS, HEAD_W)
    return (y_p.reshape(bp, seq, D_MODEL), y_s.reshape(bs, dec_seq, D_MODEL),
            k_p.reshape(kv_p), v_p.reshape(kv_p),
            conv_p.reshape(depth, bp, CONV_W - 1, LRU_WIDTH), h_p.reshape(depth, bp, LRU_WIDTH),
            k_s.reshape(kv_s), v_s.reshape(kv_s),
            conv_s.reshape(depth, bs, CONV_W - 1, LRU_WIDTH), h_s.reshape(depth, bs, LRU_WIDTH))
```

```python
import functools
import math

import jax
import jax.numpy as jnp
from jax import lax
from jax.experimental import pallas as pl
from jax.experimental.pallas import tpu as pltpu

F32 = jnp.float32
BF16 = jnp.bfloat16
U32 = jnp.uint32
I32 = jnp.int32

D_MODEL = 2048
ATTN_WIDTH = D_MODEL // 2
LRU_WIDTH = D_MODEL - ATTN_WIDTH
HEAD_DIM = 64
HEAD_W = 2 * HEAD_DIM
N_HEADS = ATTN_WIDTH // HEAD_W
LRU_BLOCKS = 8
LRU_BLOCK_W = LRU_WIDTH // LRU_BLOCKS
CONV_W = 4
LRU_C = 8.0
REL_BUCKETS = 32
REL_EXACT = REL_BUCKETS // 2
REL_MAX_DIST = 128
N_EXPERTS = 32
TOP_K = 4
D_FF = D_MODEL
SWIGLU_LIMIT = 7.0
SWIGLU_ALPHA = 1.702
EPS = 1e-5
NEG_INF = -1e30
N_PROJ = 5
PROJ_W = 1024

LANES = 128
SUBLANES = 8
VMEM_LIMIT = 56 * 1024 * 1024
PACK_W = D_MODEL // 2
PACK_S = PACK_W // LANES
ROW_S = D_MODEL // LANES

ATTN_TILE = 256
LRU_TILE = 256
ROW_TILE = 512
MOE_TILE = 512
MOE_FF_TILE = 512
COMBINE_TILE = 256
PAGES_PER_STEP = 8


def _cparams(*sem):
    return pltpu.CompilerParams(dimension_semantics=sem, vmem_limit_bytes=VMEM_LIMIT)


def _rms(x, g):
    return x * lax.rsqrt(jnp.mean(x * x, axis=-1, keepdims=True) + EPS) * g


def _inproj_kernel(x_ref, g_ref, w_ref, q_ref, kr_ref, vr_ref, kb_ref, vb_ref, xr_ref, gate_ref, xn_sc):
    j = pl.program_id(1)
    tm = x_ref.shape[0]

    @pl.when(j == 0)
    def _():
        xn_sc[...] = _rms(x_ref[...], g_ref[...]).astype(BF16)

    acc = jnp.dot(xn_sc[...], w_ref[...], preferred_element_type=F32)

    @pl.when(j == 0)
    def _():
        q_ref[...] = (acc * (HEAD_DIM ** -0.5)).astype(BF16)

    for jj, rows_ref, bf_ref in ((1, kr_ref, kb_ref), (2, vr_ref, vb_ref)):
        @pl.when(j == jj)
        def _(rows_ref=rows_ref, bf_ref=bf_ref):
            rows_ref[...] = acc.reshape(tm, N_HEADS, HEAD_W)
            bf_ref[...] = acc.astype(BF16)

    for jj, ref in ((3, xr_ref), (4, gate_ref)):
        @pl.when(j == jj)
        def _(ref=ref):
            ref[...] = acc


def _inproj(x2, g, w_bf, tm):
    n = x2.shape[0]
    row = lambda i, j: (i, 0)
    row3 = lambda i, j: (i, 0, 0)
    flat_bf = jax.ShapeDtypeStruct((n, PROJ_W), BF16)
    flat_f32 = jax.ShapeDtypeStruct((n, PROJ_W), F32)
    heads = jax.ShapeDtypeStruct((n, N_HEADS, HEAD_W), F32)
    flat_spec = pl.BlockSpec((tm, PROJ_W), row)
    head_spec = pl.BlockSpec((tm, N_HEADS, HEAD_W), row3)
    return pl.pallas_call(
        _inproj_kernel,
        grid=(n // tm, N_PROJ),
        in_specs=[pl.BlockSpec((tm, D_MODEL), row),
                  pl.BlockSpec((1, D_MODEL), lambda i, j: (0, 0)),
                  pl.BlockSpec((D_MODEL, PROJ_W), lambda i, j: (0, j))],
        out_specs=[flat_spec, head_spec, head_spec, flat_spec, flat_spec, flat_spec, flat_spec],
        out_shape=[flat_bf, heads, heads, flat_bf, flat_bf, flat_f32, flat_f32],
        scratch_shapes=[pltpu.VMEM((tm, D_MODEL), BF16)],
        compiler_params=_cparams("parallel", "arbitrary"),
        name="inproj",
    )(x2, g, w_bf)


def _split_maps(q):
    lane = lax.broadcasted_iota(I32, q.shape, 1)
    zero = jnp.zeros_like(q)
    return jnp.concatenate([jnp.where(lane < HEAD_DIM, q, zero),
                            jnp.where(lane >= HEAD_DIM, q, zero)], axis=0)


def _qk(q2, kj):
    return lax.dot_general(q2, kj, (((1,), (1,)), ((), ())), preferred_element_type=F32)


def _prompt_attn_kernel(lam_ref, q_ref, k_ref, v_ref, t0_ref, t1_ref, g_ref, o_ref,
                        vt_sc, m_sc, l_sc, acc_sc, *, tile, out_scale):
    i = pl.program_id(2)
    n_kv = vt_sc.shape[0]

    @pl.when(i == 0)
    def _():
        for j in range(n_kv):
            vt_sc[j] = v_ref[j * tile:(j + 1) * tile, :].astype(F32).T.astype(BF16)

    q_t = _split_maps(q_ref[...]).astype(F32).T.astype(BF16)
    m_sc[...] = jnp.full(m_sc.shape, NEG_INF, F32)
    l_sc[...] = jnp.zeros(l_sc.shape, F32)
    acc_sc[...] = jnp.zeros(acc_sc.shape, F32)

    def block(first, count, bias_t):
        start = pl.multiple_of(first * tile, tile)
        kj = k_ref[pl.ds(start, count * tile), :]
        s = jnp.dot(kj, q_t, preferred_element_type=F32)
        if bias_t is not None:
            s = s + jnp.concatenate([bias_t, bias_t], axis=1)
        m_prev = m_sc[...]
        m_new = jnp.maximum(m_prev, jnp.max(s, axis=0, keepdims=True))
        alpha = jnp.exp(m_prev - m_new)
        p = jnp.exp(s - m_new)
        l_sc[...] = alpha * l_sc[...] + jnp.sum(p, axis=0, keepdims=True)
        pb = p.astype(BF16)
        pv = jnp.dot(vt_sc[first], pb[0:tile], preferred_element_type=F32)
        for u in range(1, count):
            pv = pv + jnp.dot(vt_sc[first + u], pb[u * tile:(u + 1) * tile], preferred_element_type=F32)
        acc_sc[...] = alpha * acc_sc[...] + pv
        m_sc[...] = m_new

    n_far = i - 1

    def pair_body(jj, c):
        block(2 * jj, 2, None)
        return c

    lax.fori_loop(0, jnp.maximum(n_far, 0) // 2, pair_body, 0)

    @pl.when(jnp.logical_and(n_far > 0, n_far % 2 == 1))
    def _():
        block(n_far - 1, 1, None)

    @pl.when(i >= 1)
    def _():
        block(i - 1, 1, t1_ref[0])

    block(i, 1, t0_ref[0])

    o = acc_sc[...] / l_sc[...]
    d = o[:, :tile] - lam_ref[0] * o[:, tile:]
    y = d * lax.rsqrt(jnp.mean(d * d, axis=0, keepdims=True) + EPS) * out_scale
    o_ref[...] = (y.T * g_ref[...]).astype(o_ref.dtype)


def _prompt_attention(q, k_bf, v_bf, t0, t1, lam, subln_g, batch, seq, out_scale):
    tile = ATTN_TILE
    nq = seq // tile
    kern = functools.partial(_prompt_attn_kernel, tile=tile, out_scale=out_scale)
    qmap = lambda b, h, i: (b * nq + i, h)
    kvmap = lambda b, h, i: (b, h)
    tmap = lambda b, h, i: (h, 0, 0)
    return pl.pallas_call(
        kern,
        grid=(batch, N_HEADS, nq),
        in_specs=[pl.BlockSpec(memory_space=pltpu.SMEM),
                  pl.BlockSpec((tile, HEAD_W), qmap),
                  pl.BlockSpec((seq, HEAD_W), kvmap),
                  pl.BlockSpec((seq, HEAD_W), kvmap),
                  pl.BlockSpec((1, tile, tile), tmap),
                  pl.BlockSpec((1, tile, tile), tmap),
                  pl.BlockSpec((1, HEAD_W), lambda b, h, i: (0, 0))],
        out_specs=pl.BlockSpec((tile, HEAD_W), qmap),
        out_shape=jax.ShapeDtypeStruct((batch * seq, ATTN_WIDTH), BF16),
        scratch_shapes=[pltpu.VMEM((nq, HEAD_W, tile), BF16),
                        pltpu.VMEM((1, 2 * tile), F32), pltpu.VMEM((1, 2 * tile), F32),
                        pltpu.VMEM((HEAD_W, 2 * tile), F32)],
        compiler_params=_cparams("parallel", "parallel", "arbitrary"),
        name="prompt_attn",
    )(lam, q, k_bf, v_bf, t0, t1, subln_g)


def _softmax_update(s_list, v_list, m_sc, l_sc, acc_sc):
    m_prev = m_sc[...]
    m_new = m_prev
    for s in s_list:
        m_new = jnp.maximum(m_new, jnp.max(s, axis=-1, keepdims=True))
    alpha = jnp.exp(m_prev - m_new)
    l_new = alpha * l_sc[...]
    acc = alpha * acc_sc[...]
    for s, vj in zip(s_list, v_list):
        p = jnp.exp(s - m_new)
        l_new = l_new + jnp.sum(p, axis=-1, keepdims=True)
        acc = acc + jnp.dot(p.astype(BF16), vj, preferred_element_type=F32)
    l_sc[...] = l_new
    acc_sc[...] = acc
    m_sc[...] = m_new


def _sample_attn_kernel(pt_ref, lam_ref, q_ref, *refs, n_pp, out_scale):
    k_refs = refs[:n_pp]
    v_refs = refs[n_pp:2 * n_pp]
    tab_ref, ks_ref, vs_ref, tabs_ref, g_ref, o_ref, m_sc, l_sc, acc_sc = refs[2 * n_pp:]
    s_idx = pl.program_id(1)
    cols = k_refs[0].shape[1]

    @pl.when(s_idx == 0)
    def _():
        m_sc[...] = jnp.full(m_sc.shape, NEG_INF, F32)
        l_sc[...] = jnp.zeros(l_sc.shape, F32)
        acc_sc[...] = jnp.zeros(acc_sc.shape, F32)

    q2 = _split_maps(q_ref[0])
    s_list = [_qk(q2, k_refs[pp][0].astype(BF16)) + tab_ref[0, :, pp * cols:(pp + 1) * cols]
              for pp in range(n_pp)]
    _softmax_update(s_list, [v_refs[pp][0].astype(BF16) for pp in range(n_pp)], m_sc, l_sc, acc_sc)

    @pl.when(s_idx == pl.num_programs(1) - 1)
    def _():
        s = _qk(q2, ks_ref[0].astype(BF16)) + tabs_ref[...]
        _softmax_update([s], [vs_ref[0].astype(BF16)], m_sc, l_sc, acc_sc)
        o = acc_sc[...] / l_sc[...]
        r = o.shape[0] // 2
        d = o[:r] - lam_ref[0] * o[r:]
        o_ref[0] = (_rms(d, g_ref[...]) * out_scale).astype(o_ref.dtype)


def _sample_attention(q3, cache_k3, cache_v3, page_table, tab, k_self, v_self, tab_self, lam, subln_g,
                      out_scale):
    nb, rows, _ = q3.shape
    n_pages = page_table.shape[1]
    n_pp = PAGES_PER_STEP
    n_steps = n_pages // n_pp
    cols = cache_k3.shape[1]
    kern = functools.partial(_sample_attn_kernel, n_pp=n_pp, out_scale=out_scale)

    def page_map(pp):
        return lambda b, s, pt: (pt[b, s * n_pp + pp], 0, 0)

    page_specs = [pl.BlockSpec((1, cols, HEAD_W), page_map(pp)) for pp in range(n_pp)]
    bmap = lambda b, s, pt: (b, 0, 0)
    grid_spec = pltpu.PrefetchScalarGridSpec(
        num_scalar_prefetch=1,
        grid=(nb, n_steps),
        in_specs=[pl.BlockSpec(memory_space=pltpu.SMEM),
                  pl.BlockSpec((1, rows, HEAD_W), bmap)]
                 + page_specs + page_specs
                 + [pl.BlockSpec((1, 2 * rows, n_pp * cols),
                                 lambda b, s, pt: (jnp.where(s == n_steps - 1, 1, 0), 0, 0)),
                    pl.BlockSpec((1, HEAD_W, HEAD_W), bmap),
                    pl.BlockSpec((1, HEAD_W, HEAD_W), bmap),
                    pl.BlockSpec((2 * rows, HEAD_W), lambda b, s, pt: (0, 0)),
                    pl.BlockSpec((1, HEAD_W), lambda b, s, pt: (0, 0))],
        out_specs=pl.BlockSpec((1, rows, HEAD_W), bmap),
        scratch_shapes=[pltpu.VMEM((2 * rows, 1), F32), pltpu.VMEM((2 * rows, 1), F32),
                        pltpu.VMEM((2 * rows, HEAD_W), F32)],
    )
    return pl.pallas_call(
        kern,
        grid_spec=grid_spec,
        out_shape=jax.ShapeDtypeStruct((nb, rows, HEAD_W), BF16),
        compiler_params=_cparams("parallel", "arbitrary"),
        name="sample_attn",
    )(page_table, lam, q3, *([cache_k3] * n_pp), *([cache_v3] * n_pp), tab, k_self, v_self, tab_self,
      subln_g)


def _softplus(x):
    return jnp.maximum(x, 0.0) + jnp.log1p(jnp.exp(-jnp.abs(x)))


def _gelu_tanh(x):
    return 0.5 * x * (1.0 + jnp.tanh(math.sqrt(2.0 / math.pi) * (x + 0.044715 * x * x * x)))


def _block_gate(cb, w_ref, b):
    parts = [jnp.dot(cb[:, n * LRU_BLOCK_W:(n + 1) * LRU_BLOCK_W], w_ref[n], preferred_element_type=F32)
             for n in range(LRU_BLOCKS)]
    return jax.nn.sigmoid(jnp.concatenate(parts, axis=-1) + b)


def _lru_coeffs(conv, wrg_ref, brg, wig_ref, big, lam):
    cb = conv.astype(BF16)
    rg = _block_gate(cb, wrg_ref, brg)
    ig = _block_gate(cb, wig_ref, big)
    log_a = -LRU_C * rg * _softplus(-lam)
    a = jnp.exp(log_a)
    bx = jnp.sqrt(jnp.tanh(-log_a) * (1.0 + a * a)) * ig * conv
    return a, bx


def _prompt_lru_kernel(xr_ref, gate_ref, cw_ref, cb_ref, wrg_ref, brg_ref, wig_ref, big_ref, lam_ref,
                       y_ref, conv_ref, h_ref, xc_sc, h_sc, *, tile):
    t = pl.program_id(1)
    pad = SUBLANES

    @pl.when(t == 0)
    def _():
        xc_sc[0:pad, :] = jnp.zeros((pad, LRU_WIDTH), F32)
        h_sc[...] = jnp.zeros(h_sc.shape, F32)

    xr = xr_ref[...]
    xc_sc[pad:pad + tile, :] = xr
    conv = cb_ref[...]
    for j in range(CONV_W):
        off = pad - (CONV_W - 1) + j
        conv = conv + xc_sc[off:off + tile, :] * cw_ref[j:j + 1, :]
    xc_sc[0:pad, :] = xr[tile - pad:, :]

    a, bx = _lru_coeffs(conv, wrg_ref, brg_ref[...], wig_ref, big_ref[...], lam_ref[...])

    row = lax.broadcasted_iota(I32, a.shape, 0)
    d = 1
    while d < tile:
        keep = row >= d
        a_sh = pltpu.roll(a, d, 0)
        b_sh = pltpu.roll(bx, d, 0)
        bx = jnp.where(keep, a * b_sh + bx, bx)
        a = jnp.where(keep, a * a_sh, a)
        d *= 2
    h = bx + a * h_sc[...]
    h_sc[...] = h[tile - 1:tile, :]
    y_ref[...] = (h * _gelu_tanh(gate_ref[...])).astype(y_ref.dtype)

    @pl.when(t == pl.num_programs(1) - 1)
    def _():
        conv_ref[0] = xc_sc[pad + tile - (CONV_W - 1):pad + tile, :]
        h_ref[0] = h[tile - 1:tile, :]


def _prompt_lru(xr, gate, conv_w, conv_b, w_rg, b_rg, w_ig, b_ig, lru_lambda, batch, seq):
    tile = LRU_TILE
    nt = seq // tile
    kern = functools.partial(_prompt_lru_kernel, tile=tile)
    rmap = lambda b, t: (b * nt + t, 0)
    full2 = lambda b, t: (0, 0)
    full3 = lambda b, t: (0, 0, 0)
    bmap = lambda b, t: (b, 0, 0)
    vec = pl.BlockSpec((1, LRU_WIDTH), full2)
    wspec = pl.BlockSpec((LRU_BLOCKS, LRU_BLOCK_W, LRU_BLOCK_W), full3)
    return pl.pallas_call(
        kern,
        grid=(batch, nt),
        in_specs=[pl.BlockSpec((tile, LRU_WIDTH), rmap), pl.BlockSpec((tile, LRU_WIDTH), rmap),
                  pl.BlockSpec((CONV_W, LRU_WIDTH), full2), vec, wspec, vec, wspec, vec, vec],
        out_specs=[pl.BlockSpec((tile, LRU_WIDTH), rmap),
                   pl.BlockSpec((1, CONV_W - 1, LRU_WIDTH), bmap),
                   pl.BlockSpec((1, 1, LRU_WIDTH), bmap)],
        out_shape=[jax.ShapeDtypeStruct((batch * seq, LRU_WIDTH), BF16),
                   jax.ShapeDtypeStruct((batch, CONV_W - 1, LRU_WIDTH), F32),
                   jax.ShapeDtypeStruct((batch, 1, LRU_WIDTH), F32)],
        scratch_shapes=[pltpu.VMEM((tile + SUBLANES, LRU_WIDTH), F32), pltpu.VMEM((1, LRU_WIDTH), F32)],
        compiler_params=_cparams("parallel", "arbitrary"),
        name="prompt_lru",
    )(xr, gate, conv_w, conv_b, w_rg, b_rg, w_ig, b_ig, lru_lambda)


def _sample_lru_kernel(xr_ref, gate_ref, cprev_ref, hprev_ref, cw_ref, cb_ref, wrg_ref, brg_ref,
                       wig_ref, big_ref, lam_ref, y_ref, conv_ref, h_ref, *, steps):
    xcat = [cprev_ref[j] for j in range(CONV_W - 1)] + [xr_ref[t] for t in range(steps)]
    h = hprev_ref[...]
    for t in range(steps):
        conv = cb_ref[...]
        for j in range(CONV_W):
            conv = conv + xcat[t + j] * cw_ref[j:j + 1, :]
        a, bx = _lru_coeffs(conv, wrg_ref, brg_ref[...], wig_ref, big_ref[...], lam_ref[...])
        h = a * h + bx
        y_ref[t] = (h * _gelu_tanh(gate_ref[t])).astype(y_ref.dtype)
    for j in range(CONV_W - 1):
        conv_ref[j] = xcat[steps + j]
    h_ref[...] = h


def _sample_lru(xr_t, gate_t, cprev_t, hprev, conv_w, conv_b, w_rg, b_rg, w_ig, b_ig, lru_lambda):
    steps, nb, _ = xr_t.shape
    kern = functools.partial(_sample_lru_kernel, steps=steps)
    return pl.pallas_call(
        kern,
        out_shape=[jax.ShapeDtypeStruct((steps, nb, LRU_WIDTH), BF16),
                   jax.ShapeDtypeStruct((CONV_W - 1, nb, LRU_WIDTH), F32),
                   jax.ShapeDtypeStruct((nb, LRU_WIDTH), F32)],
        compiler_params=pltpu.CompilerParams(vmem_limit_bytes=VMEM_LIMIT),
        name="sample_lru",
    )(xr_t, gate_t, cprev_t, hprev, conv_w, conv_b, w_rg, b_rg, w_ig, b_ig, lru_lambda)


def _split_bf16(x):
    hi = x.astype(BF16)
    return hi, (x - hi.astype(F32)).astype(BF16)


def _outproj_kernel(a_ref, l_ref, w_ref, x_ref, g_ref, rw_ref, rb_ref, xmid_ref, xp_ref, logit_ref):
    tm = x_ref.shape[0]
    y = jnp.dot(a_ref[...], w_ref[0:ATTN_WIDTH, :], preferred_element_type=F32)
    y = y + jnp.dot(l_ref[...], w_ref[ATTN_WIDTH:, :], preferred_element_type=F32)
    xm = x_ref[...] + y
    xmid_ref[...] = xm
    xn = _rms(xm, g_ref[...])
    xn_hi, xn_lo = _split_bf16(xn)
    bits = pltpu.bitcast(xn_hi.astype(F32), U32)
    words = bits[:, PACK_W:] | (bits[:, :PACK_W] >> 16)
    xp_ref[...] = words.reshape(tm, PACK_S, LANES)
    rw_hi, rw_lo = _split_bf16(rw_ref[...])
    logits = (jnp.dot(xn_hi, rw_hi, preferred_element_type=F32)
              + jnp.dot(xn_lo, rw_hi, preferred_element_type=F32)
              + jnp.dot(xn_hi, rw_lo, preferred_element_type=F32))
    logit_ref[...] = logits + rb_ref[...]


def _outproj(attn, lru, w_out_bf, x2, g, router_w, router_b, tm):
    n = x2.shape[0]
    row = lambda i: (i, 0)
    full = lambda i: (0, 0)
    return pl.pallas_call(
        _outproj_kernel,
        grid=(n // tm,),
        in_specs=[pl.BlockSpec((tm, ATTN_WIDTH), row), pl.BlockSpec((tm, LRU_WIDTH), row),
                  pl.BlockSpec((D_MODEL, D_MODEL), full, pipeline_mode=pl.Buffered(1)),
                  pl.BlockSpec((tm, D_MODEL), row),
                  pl.BlockSpec((1, D_MODEL), full), pl.BlockSpec((D_MODEL, N_EXPERTS), full),
                  pl.BlockSpec((1, N_EXPERTS), full)],
        out_specs=[pl.BlockSpec((tm, D_MODEL), row), pl.BlockSpec((tm, PACK_S, LANES), lambda i: (i, 0, 0)),
                   pl.BlockSpec((tm, N_EXPERTS), row)],
        out_shape=[jax.ShapeDtypeStruct((n, D_MODEL), F32), jax.ShapeDtypeStruct((n, PACK_S, LANES), U32),
                   jax.ShapeDtypeStruct((n, N_EXPERTS), F32)],
        compiler_params=_cparams("parallel"),
        name="outproj",
    )(attn, lru, w_out_bf, x2, g, router_w, router_b)


def _route_kernel(logit_ref, base_ref, idx_ref, gate_ref, rank_ref, cnt_ref, cnt_sc):
    i = pl.program_id(0)
    tr = logit_ref.shape[0]

    @pl.when(i == 0)
    def _():
        cnt_sc[...] = base_ref[...]

    logits = logit_ref[...]
    lane = lax.broadcasted_iota(I32, logits.shape, 1)
    vals, firsts, hots = [], [], []
    for _ in range(TOP_K):
        mx = jnp.max(logits, axis=-1, keepdims=True)
        first = jnp.min(jnp.where(logits == mx, lane, N_EXPERTS), axis=-1, keepdims=True)
        hot = lane == first
        vals.append(mx)
        firsts.append(first)
        hots.append(hot)
        logits = jnp.where(hot, -jnp.inf, logits)

    member = jnp.zeros(logits.shape, F32)
    for hot in hots:
        member = member + hot.astype(F32)
    r_i = lax.broadcasted_iota(I32, (tr, tr), 0)
    c_i = lax.broadcasted_iota(I32, (tr, tr), 1)
    earlier = jnp.where(c_i < r_i, 1.0, 0.0).astype(BF16)
    before = cnt_sc[...] + jnp.dot(earlier, member.astype(BF16), preferred_element_type=F32)

    exps = [jnp.exp(v - vals[0]) for v in vals]
    denom = exps[0]
    for e in exps[1:]:
        denom = denom + e
    for k in range(TOP_K):
        idx_ref[:, k:k + 1] = firsts[k]
        gate_ref[:, k:k + 1] = exps[k] / denom
        rank = jnp.sum(jnp.where(hots[k], before, 0.0), axis=-1, keepdims=True)
        rank_ref[:, k:k + 1] = rank.astype(I32)
    cnt_sc[...] = cnt_sc[...] + jnp.sum(member, axis=0, keepdims=True)

    @pl.when(i == pl.num_programs(0) - 1)
    def _():
        cnt_ref[...] = cnt_sc[...]


def _route(logits, base_counts, tr):
    n = logits.shape[0]
    row = lambda i: (i, 0)
    full = lambda i: (0, 0)
    k_spec = pl.BlockSpec((tr, TOP_K), row)
    return pl.pallas_call(
        _route_kernel,
        grid=(n // tr,),
        in_specs=[pl.BlockSpec((tr, N_EXPERTS), row), pl.BlockSpec((1, N_EXPERTS), full)],
        out_specs=[k_spec, k_spec, k_spec, pl.BlockSpec((1, N_EXPERTS), full)],
        out_shape=[jax.ShapeDtypeStruct((n, TOP_K), I32), jax.ShapeDtypeStruct((n, TOP_K), F32),
                   jax.ShapeDtypeStruct((n, TOP_K), I32), jax.ShapeDtypeStruct((1, N_EXPERTS), F32)],
        scratch_shapes=[pltpu.VMEM((1, N_EXPERTS), F32)],
        compiler_params=_cparams("arbitrary"),
        name="route",
    )(logits, base_counts)


def _dispatch_kernel(pos_ref, x_ref, xs_in_ref, xs_ref, sem):
    del xs_in_ref
    tm = x_ref.shape[0]

    def issue(r, c):
        for k in range(TOP_K):
            pltpu.make_async_copy(x_ref.at[r], xs_ref.at[pos_ref[r * TOP_K + k]], sem).start()
        return c

    lax.fori_loop(0, tm, issue, 0)
    for _ in range(TOP_K):
        pltpu.make_async_copy(x_ref, xs_ref.at[pl.ds(0, tm)], sem).wait()


def _dispatch(pos_flat, x_packed, x_sorted, tm):
    n = x_packed.shape[0]
    return pl.pallas_call(
        _dispatch_kernel,
        grid=(n // tm,),
        in_specs=[pl.BlockSpec((tm * TOP_K,), lambda i: (i,), memory_space=pltpu.SMEM),
                  pl.BlockSpec((tm, PACK_S, LANES), lambda i: (i, 0, 0)),
                  pl.BlockSpec(memory_space=pl.ANY)],
        out_specs=pl.BlockSpec(memory_space=pl.ANY),
        out_shape=jax.ShapeDtypeStruct(x_sorted.shape, x_sorted.dtype),
        scratch_shapes=[pltpu.SemaphoreType.DMA],
        input_output_aliases={2: 0},
        compiler_params=_cparams("arbitrary"),
        name="moe_dispatch",
    )(pos_flat, x_packed, x_sorted)


def _moe_kernel(te_ref, na_ref, x_ref, wg_ref, wu_ref, wd_ref, bg_ref, bu_ref, bd_ref, o_ref,
                xb_sc, acc_sc):
    i = pl.program_id(0)
    j = pl.program_id(1)
    tm = x_ref.shape[0]

    @pl.when(i < na_ref[0])
    def _():
        @pl.when(j == 0)
        def _():
            w = x_ref[...].reshape(tm, PACK_W)
            xb_sc[:, :PACK_W] = pltpu.bitcast(w << 16, F32).astype(BF16)
            xb_sc[:, PACK_W:] = pltpu.bitcast(w & jnp.uint32(0xFFFF0000), F32).astype(BF16)

        x = xb_sc[...]
        glu = jnp.dot(x, wg_ref[0], preferred_element_type=F32) + bg_ref[0]
        lin = jnp.dot(x, wu_ref[0], preferred_element_type=F32) + bu_ref[0]
        glu = jnp.minimum(glu, SWIGLU_LIMIT)
        lin = jnp.clip(lin, -SWIGLU_LIMIT, SWIGLU_LIMIT)
        act = glu * jax.nn.sigmoid(SWIGLU_ALPHA * glu) * (lin + 1.0)
        part = jnp.dot(act.astype(BF16), wd_ref[0], preferred_element_type=F32)

        @pl.when(j == 0)
        def _():
            acc_sc[...] = part + bd_ref[0]

        @pl.when(j > 0)
        def _():
            acc_sc[...] += part

        @pl.when(j == pl.num_programs(1) - 1)
        def _():
            o_ref[...] = acc_sc[...].reshape(tm, ROW_S, LANES)

    @pl.when(jnp.logical_and(i >= na_ref[0], j == pl.num_programs(1) - 1))
    def _():
        o_ref[...] = jnp.zeros(o_ref.shape, F32)


def _moe_ffn(x_sorted, tile_expert, n_active, w_gate_up, b_gate_up, w_down, b_down):
    n_slots = x_sorted.shape[0]
    tm, tf = MOE_TILE, MOE_FF_TILE
    n_tiles = n_slots // tm
    n_f = D_FF // tf
    last_f = n_f - 1

    def fidx(i, j, na):
        return jnp.where(i < na[0], j, last_f)

    def rowidx(i, na):
        return jnp.minimum(i, na[0] - 1)

    grid_spec = pltpu.PrefetchScalarGridSpec(
        num_scalar_prefetch=2,
        grid=(n_tiles, n_f),
        in_specs=[pl.BlockSpec((tm, PACK_S, LANES), lambda i, j, te, na: (rowidx(i, na), 0, 0)),
                  pl.BlockSpec((1, D_MODEL, tf), lambda i, j, te, na: (te[i], 0, fidx(i, j, na))),
                  pl.BlockSpec((1, D_MODEL, tf), lambda i, j, te, na: (te[i], 0, n_f + fidx(i, j, na))),
                  pl.BlockSpec((1, tf, D_MODEL), lambda i, j, te, na: (te[i], fidx(i, j, na), 0)),
                  pl.BlockSpec((1, 1, tf), lambda i, j, te, na: (te[i], 0, fidx(i, j, na))),
                  pl.BlockSpec((1, 1, tf), lambda i, j, te, na: (te[i], 0, n_f + fidx(i, j, na))),
                  pl.BlockSpec((1, 1, D_MODEL), lambda i, j, te, na: (te[i], 0, 0))],
        out_specs=pl.BlockSpec((tm, ROW_S, LANES), lambda i, j, te, na: (i, 0, 0)),
        scratch_shapes=[pltpu.VMEM((tm, D_MODEL), BF16), pltpu.VMEM((tm, D_MODEL), F32)],
    )
    bgu3 = b_gate_up.reshape(N_EXPERTS, 1, 2 * D_FF)
    bd3 = b_down.reshape(N_EXPERTS, 1, D_MODEL)
    return pl.pallas_call(
        _moe_kernel,
        grid_spec=grid_spec,
        out_shape=jax.ShapeDtypeStruct((n_slots, ROW_S, LANES), F32),
        compiler_params=_cparams("arbitrary", "arbitrary"),
        name="moe_ffn",
    )(tile_expert, n_active, x_sorted, w_gate_up, w_gate_up, w_down, bgu3, bgu3, bd3)


def _combine_kernel(pos_ref, gate_ref, x_ref, g_ref, y_hbm, o_ref, ybuf, sem):
    tm = x_ref.shape[0]

    def issue(r, c):
        for k in range(TOP_K):
            pltpu.make_async_copy(y_hbm.at[pos_ref[r * TOP_K + k]], ybuf.at[k, r], sem).start()
        return c

    lax.fori_loop(0, tm, issue, 0)
    for k in range(TOP_K):
        pltpu.make_async_copy(y_hbm.at[pl.ds(0, tm)], ybuf.at[k], sem).wait()

    gates = gate_ref[...]
    out = x_ref[...]
    for k in range(TOP_K):
        out = out + gates[:, k:k + 1] * ybuf[k].reshape(tm, D_MODEL)
    o_ref[...] = _rms(out, g_ref[...])


def _combine(pos_flat, gates, x_mid, g, y_sorted, tm):
    n = x_mid.shape[0]
    row = lambda i: (i, 0)
    return pl.pallas_call(
        _combine_kernel,
        grid=(n // tm,),
        in_specs=[pl.BlockSpec((tm * TOP_K,), lambda i: (i,), memory_space=pltpu.SMEM),
                  pl.BlockSpec((tm, TOP_K), row),
                  pl.BlockSpec((tm, D_MODEL), row),
                  pl.BlockSpec((1, D_MODEL), lambda i: (0, 0)),
                  pl.BlockSpec(memory_space=pl.ANY)],
        out_specs=pl.BlockSpec((tm, D_MODEL), row),
        out_shape=jax.ShapeDtypeStruct((n, D_MODEL), F32),
        scratch_shapes=[pltpu.VMEM((TOP_K, tm, ROW_S, LANES), F32), pltpu.SemaphoreType.DMA],
        compiler_params=_cparams("arbitrary"),
        name="moe_combine",
    )(pos_flat, gates, x_mid, g, y_sorted)


def _bias_of_distance(rel_table, n):
    nf = jnp.maximum(n, 1).astype(F32)
    large = REL_EXACT + (jnp.log(nf / REL_EXACT) / math.log(REL_MAX_DIST / REL_EXACT)
                         * (REL_BUCKETS - REL_EXACT)).astype(I32)
    bucket = jnp.where(n < REL_EXACT, n, jnp.minimum(large, REL_BUCKETS - 1))
    return rel_table[bucket].astype(F32)


def _prompt_bias_tables(rel_table, tile):
    far = rel_table[REL_BUCKETS - 1].astype(F32)
    d = jnp.arange(tile)[None, :] - jnp.arange(tile)[:, None]
    diag = jnp.where((d >= 0)[..., None], _bias_of_distance(rel_table, jnp.maximum(d, 0)) - far, NEG_INF)
    near = _bias_of_distance(rel_table, d + tile) - far
    return jnp.transpose(diag, (2, 0, 1)), jnp.transpose(near, (2, 0, 1))


def _sample_bias_tables(rel_table, dec_seq, past_len, page_size):
    heads = jnp.arange(N_HEADS)
    same_head = heads[:, None] == heads[None, :]

    def table(dist, valid):
        b = _bias_of_distance(rel_table, jnp.maximum(dist, 0))
        b = jnp.transpose(b, (0, 2, 1))
        full = jnp.where(same_head[None, :, None, :] & valid[:, None, :, None],
                         b[:, :, :, None], NEG_INF)
        full = full.reshape(dec_seq * N_HEADS, -1)
        return jnp.concatenate([full, full], axis=0)

    qpos = past_len + jnp.arange(dec_seq)
    span = PAGES_PER_STEP * page_size
    always = jnp.ones((dec_seq, span), bool)
    k_last = past_len - span + jnp.arange(span)
    tab_far = table(jnp.full((dec_seq, span), REL_MAX_DIST), always)
    tab_last = table(qpos[:, None] - k_last[None, :], always)
    n_self = HEAD_W // N_HEADS
    tk = jnp.arange(n_self)
    dist = jnp.arange(dec_seq)[:, None] - tk[None, :]
    tab_self = table(dist, (dist >= 0) & (tk[None, :] < dec_seq))
    return jnp.stack([tab_far, tab_last]), tab_self


def _tile_plan(counts, n_assign, tm):
    counts = counts.reshape(N_EXPERTS).astype(I32)
    padded = (counts + tm - 1) // tm * tm
    pad_end = jnp.cumsum(padded)
    pad_start = pad_end - padded
    n_tiles = n_assign // tm + N_EXPERTS
    n_active = (pad_end[-1] // tm).astype(I32)
    tile_start = jnp.minimum(jnp.arange(n_tiles, dtype=I32), n_active - 1) * tm
    tile_expert = jnp.minimum(jnp.sum((pad_end[None, :] <= tile_start[:, None]).astype(I32), axis=1),
                              N_EXPERTS - 1)
    return pad_start, tile_expert, n_active.reshape(1), n_tiles


def _slots(pad_start, idx, rank):
    hot = idx[..., None] == jnp.arange(N_EXPERTS, dtype=I32)
    return (jnp.sum(jnp.where(hot, pad_start, 0), axis=-1) + rank).reshape(-1)


def kernel(x_prompt, x_sample, cache_k, cache_v, state_conv, state_h, page_table, rel_bias, norm_mix_g, w_in, lam_q1, lam_k1, lam_q2, lam_k2, subln_g, conv_w, conv_b, w_rg, b_rg, w_ig, b_ig, lru_lambda, w_out, norm_ffn_g, router_w, router_b, w_gate_up, b_gate_up, w_down, b_down, norm_final_g):
    depth = w_in.shape[0]
    assert depth == 1
    layer = 0
    bp, seq, _ = x_prompt.shape
    bs, dec_seq, _ = x_sample.shape
    n_pool, page_size = cache_k.shape[1], cache_k.shape[2]
    past_len = page_table.shape[1] * page_size
    n_p, n_s = bp * seq, bs * dec_seq
    assert ATTN_TILE >= REL_MAX_DIST and PAGES_PER_STEP * page_size >= REL_MAX_DIST

    lam_init = 0.8 - 0.6 * math.exp(-0.3 * layer)
    out_scale = 1.0 - lam_init
    lam = (jnp.exp(jnp.sum(lam_q1[layer] * lam_k1[layer])) - jnp.exp(jnp.sum(lam_q2[layer] * lam_k2[layer]))
           + lam_init).reshape(1).astype(F32)

    g_mix = norm_mix_g[layer].reshape(1, D_MODEL)
    g_ffn = norm_ffn_g[layer].reshape(1, D_MODEL)
    g_fin = norm_final_g.reshape(1, D_MODEL)
    g_sub = subln_g[layer].reshape(1, HEAD_W)
    w_in_bf = w_in[layer].astype(BF16)
    w_out_bf = w_out[layer].astype(BF16)
    lru_args = (conv_w[layer], conv_b[layer].reshape(1, LRU_WIDTH),
                w_rg[layer].astype(BF16), b_rg[layer].reshape(1, LRU_WIDTH),
                w_ig[layer].astype(BF16), b_ig[layer].reshape(1, LRU_WIDTH),
                lru_lambda[layer].reshape(1, LRU_WIDTH))
    rw = router_w[layer]
    rb = router_b[layer].reshape(1, N_EXPERTS)

    xp2 = x_prompt.reshape(n_p, D_MODEL)
    xs2 = x_sample.reshape(n_s, D_MODEL)

    q_p, k_p, v_p, kb_p, vb_p, xr_p, gate_p = _inproj(xp2, g_mix, w_in_bf, ROW_TILE)
    t0, t1 = _prompt_bias_tables(rel_bias, ATTN_TILE)
    attn_p = _prompt_attention(q_p, kb_p, vb_p, t0, t1, lam, g_sub, bp, seq, out_scale)
    lru_p, conv_p, h_p = _prompt_lru(xr_p, gate_p, *lru_args, bp, seq)
    xmid_p, xpk_p, logit_p = _outproj(attn_p, lru_p, w_out_bf, xp2, g_ffn, rw, rb, ROW_TILE)

    q_s, k_s, v_s, _, _, xr_s, gate_s = _inproj(xs2, g_mix, w_in_bf, n_s)
    tab, tab_self = _sample_bias_tables(rel_bias, dec_seq, past_len, page_size)
    rows = dec_seq * N_HEADS
    pad_rows = HEAD_W - rows

    def self_rows(a):
        a3 = a.reshape(bs, rows, HEAD_W)
        return jnp.concatenate([a3, jnp.zeros((bs, pad_rows, HEAD_W), a.dtype)], axis=1)

    attn_s = _sample_attention(
        q_s.reshape(bs, rows, HEAD_W),
        cache_k[layer].reshape(n_pool, page_size * N_HEADS, HEAD_W),
        cache_v[layer].reshape(n_pool, page_size * N_HEADS, HEAD_W),
        page_table, tab, self_rows(k_s), self_rows(v_s), tab_self, lam, g_sub, out_scale)
    attn_s = attn_s.reshape(n_s, ATTN_WIDTH)

    def tmajor(a):
        return jnp.transpose(a.reshape(bs, dec_seq, LRU_WIDTH), (1, 0, 2))

    lru_s_t, conv_s_t, h_s = _sample_lru(tmajor(xr_s), tmajor(gate_s),
                                         jnp.transpose(state_conv[layer], (1, 0, 2)), state_h[layer],
                                         *lru_args)
    lru_s = jnp.transpose(lru_s_t, (1, 0, 2)).reshape(n_s, LRU_WIDTH)
    conv_s = jnp.transpose(conv_s_t, (1, 0, 2))
    xmid_s, xpk_s, logit_s = _outproj(attn_s, lru_s, w_out_bf, xs2, g_ffn, rw, rb, n_s)

    idx_p, gates_p, rank_p, cnt_p = _route(logit_p, jnp.zeros((1, N_EXPERTS), F32), ROW_TILE)
    idx_s, gates_s, rank_s, cnt = _route(logit_s, cnt_p, n_s)
    n_assign = (n_p + n_s) * TOP_K
    pad_start, tile_expert, n_active, n_tiles = _tile_plan(cnt, n_assign, MOE_TILE)
    pos_p = _slots(pad_start, idx_p, rank_p)
    pos_s = _slots(pad_start, idx_s, rank_s)
    x_sorted = jnp.zeros((n_tiles * MOE_TILE, PACK_S, LANES), U32)
    x_sorted = _dispatch(pos_p, xpk_p, x_sorted, ROW_TILE)
    x_sorted = _dispatch(pos_s, xpk_s, x_sorted, n_s)
    y_sorted = _moe_ffn(x_sorted, tile_expert, n_active, w_gate_up[layer].astype(BF16), b_gate_up[layer],
                        w_down[layer].astype(BF16), b_down[layer])
    y_p = _combine(pos_p, gates_p, xmid_p, g_fin, y_sorted, COMBINE_TILE)
    y_s = _combine(pos_s, gates_s, xmid_s, g_fin, y_sorted, n_s)

    kv_p = (depth, bp, seq, N_HEADS, HEAD_W)
    kv_s = (depth, bs, dec_seq, N_HEADS, HEAD_W)
    return (y_p.reshape(bp, seq, D_MODEL), y_s.reshape(bs, dec_seq, D_MODEL),
            k_p.reshape(kv_p), v_p.reshape(kv_p),
            conv_p.reshape(depth, bp, CONV_W - 1, LRU_WIDTH), h_p.reshape(depth, bp, LRU_WIDTH),
            k_s.reshape(kv_s), v_s.reshape(kv_s),
            conv_s.reshape(depth, bs, CONV_W - 1, LRU_WIDTH), h_s.reshape(depth, bs, LRU_WIDTH))
```

```python
import functools
import math

import jax
import jax.numpy as jnp
from jax import lax
from jax.experimental import pallas as pl
from jax.experimental.pallas import tpu as pltpu

F32 = jnp.float32
BF16 = jnp.bfloat16
U32 = jnp.uint32
I32 = jnp.int32

D_MODEL = 2048
ATTN_WIDTH = D_MODEL // 2
LRU_WIDTH = D_MODEL - ATTN_WIDTH
HEAD_DIM = 64
HEAD_W = 2 * HEAD_DIM
N_HEADS = ATTN_WIDTH // HEAD_W
LRU_BLOCKS = 8
LRU_BLOCK_W = LRU_WIDTH // LRU_BLOCKS
CONV_W = 4
LRU_C = 8.0
REL_BUCKETS = 32
REL_EXACT = REL_BUCKETS // 2
REL_MAX_DIST = 128
N_EXPERTS = 32
TOP_K = 4
D_FF = D_MODEL
SWIGLU_LIMIT = 7.0
SWIGLU_ALPHA = 1.702
EPS = 1e-5
NEG_INF = -1e30
N_PROJ = 5
PROJ_W = 1024

LANES = 128
SUBLANES = 8
VMEM_LIMIT = 56 * 1024 * 1024
PACK_W = D_MODEL // 2
PACK_S = PACK_W // LANES
ROW_S = D_MODEL // LANES

ATTN_TILE = 512
LRU_TILE = 256
ROW_TILE = 512
MOE_TILE = 512
MOE_FF_TILE = 512
COMBINE_TILE = 256
PAGES_PER_STEP = 8


def _cparams(*sem):
    return pltpu.CompilerParams(dimension_semantics=sem, vmem_limit_bytes=VMEM_LIMIT)


def _rms(x, g):
    return x * lax.rsqrt(jnp.mean(x * x, axis=-1, keepdims=True) + EPS) * g


def _bf16_bits(x):
    return pltpu.bitcast(x.astype(BF16).astype(F32), U32)


def _unpack_words(w):
    return (pltpu.bitcast(w << 16, F32), pltpu.bitcast(w & jnp.uint32(0xFFFF0000), F32))


def _inproj_kernel(x_ref, g_ref, w_ref, q_ref, kr_ref, vr_ref, kb_ref, vb_ref, xr_ref, gate_ref, xn_sc):
    j = pl.program_id(1)
    tm = x_ref.shape[0]

    @pl.when(j == 0)
    def _():
        xn_sc[...] = _rms(x_ref[...], g_ref[...]).astype(BF16)

    acc = jnp.dot(xn_sc[...], w_ref[...], preferred_element_type=F32)

    @pl.when(j == 0)
    def _():
        q_ref[...] = (acc * (HEAD_DIM ** -0.5)).astype(BF16)

    for jj, rows_ref, bf_ref in ((1, kr_ref, kb_ref), (2, vr_ref, vb_ref)):
        @pl.when(j == jj)
        def _(rows_ref=rows_ref, bf_ref=bf_ref):
            rows_ref[...] = acc.reshape(tm, N_HEADS, HEAD_W)
            bf_ref[...] = acc.astype(BF16)

    for jj, ref in ((3, xr_ref), (4, gate_ref)):
        @pl.when(j == jj)
        def _(ref=ref):
            ref[...] = acc


def _inproj(x2, g, w_bf, tm):
    n = x2.shape[0]
    row = lambda i, j: (i, 0)
    row3 = lambda i, j: (i, 0, 0)
    flat_bf = jax.ShapeDtypeStruct((n, PROJ_W), BF16)
    flat_f32 = jax.ShapeDtypeStruct((n, PROJ_W), F32)
    heads = jax.ShapeDtypeStruct((n, N_HEADS, HEAD_W), F32)
    flat_spec = pl.BlockSpec((tm, PROJ_W), row)
    head_spec = pl.BlockSpec((tm, N_HEADS, HEAD_W), row3)
    return pl.pallas_call(
        _inproj_kernel,
        grid=(n // tm, N_PROJ),
        in_specs=[pl.BlockSpec((tm, D_MODEL), row),
                  pl.BlockSpec((1, D_MODEL), lambda i, j: (0, 0)),
                  pl.BlockSpec((D_MODEL, PROJ_W), lambda i, j: (0, j))],
        out_specs=[flat_spec, head_spec, head_spec, flat_spec, flat_spec, flat_spec, flat_spec],
        out_shape=[flat_bf, heads, heads, flat_bf, flat_bf, flat_f32, flat_f32],
        scratch_shapes=[pltpu.VMEM((tm, D_MODEL), BF16)],
        compiler_params=_cparams("parallel", "arbitrary"),
        name="inproj",
    )(x2, g, w_bf)


def _split_maps(q):
    lane = lax.broadcasted_iota(I32, q.shape, 1)
    zero = jnp.zeros_like(q)
    return jnp.concatenate([jnp.where(lane < HEAD_DIM, q, zero),
                            jnp.where(lane >= HEAD_DIM, q, zero)], axis=0)


def _qk(q2, kj):
    return lax.dot_general(q2, kj, (((1,), (1,)), ((), ())), preferred_element_type=F32)


def _prompt_attn_kernel(lam_ref, q_ref, k_ref, v_ref, t0_ref, t1_ref, g_ref, wa_ref, wb_ref,
                        o_ref, wa_out_ref, wb_out_ref, vt_sc, m_sc, l_sc, acc_sc, *, tile, out_scale):
    i = pl.program_id(2)
    n_kv = vt_sc.shape[0]

    wa_out_ref[...] = wa_ref[...].astype(BF16)
    wb_out_ref[...] = wb_ref[...].astype(BF16)

    @pl.when(i == 0)
    def _():
        for j in range(n_kv):
            vt_sc[j] = v_ref[j * tile:(j + 1) * tile, :].astype(F32).T.astype(BF16)

    q_t = _split_maps(q_ref[...]).astype(F32).T.astype(BF16)
    m_sc[...] = jnp.full(m_sc.shape, NEG_INF, F32)
    l_sc[...] = jnp.zeros(l_sc.shape, F32)
    acc_sc[...] = jnp.zeros(acc_sc.shape, F32)

    def block(first, count, bias_t):
        start = pl.multiple_of(first * tile, tile)
        kj = k_ref[pl.ds(start, count * tile), :]
        s = jnp.dot(kj, q_t, preferred_element_type=F32)
        if bias_t is not None:
            s = s + jnp.concatenate([bias_t, bias_t], axis=1)
        m_prev = m_sc[...]
        m_new = jnp.maximum(m_prev, jnp.max(s, axis=0, keepdims=True))
        alpha = jnp.exp(m_prev - m_new)
        p = jnp.exp(s - m_new)
        l_sc[...] = alpha * l_sc[...] + jnp.sum(p, axis=0, keepdims=True)
        pb = p.astype(BF16)
        pv = jnp.dot(vt_sc[first], pb[0:tile], preferred_element_type=F32)
        for u in range(1, count):
            pv = pv + jnp.dot(vt_sc[first + u], pb[u * tile:(u + 1) * tile], preferred_element_type=F32)
        acc_sc[...] = alpha * acc_sc[...] + pv
        m_sc[...] = m_new

    n_far = i - 1

    def pair_body(jj, c):
        block(2 * jj, 2, None)
        return c

    lax.fori_loop(0, jnp.maximum(n_far, 0) // 2, pair_body, 0)

    @pl.when(jnp.logical_and(n_far > 0, n_far % 2 == 1))
    def _():
        block(n_far - 1, 1, None)

    @pl.when(i >= 1)
    def _():
        block(i - 1, 1, t1_ref[0])

    block(i, 1, t0_ref[0])

    o = acc_sc[...] / l_sc[...]
    d = o[:, :tile] - lam_ref[0] * o[:, tile:]
    y = d * lax.rsqrt(jnp.mean(d * d, axis=0, keepdims=True) + EPS) * out_scale
    o_ref[...] = (y.T * g_ref[...]).astype(o_ref.dtype)


def _prompt_attention(q, k_bf, v_bf, t0, t1, lam, subln_g, w_a, w_b, batch, seq, out_scale):
    tile = ATTN_TILE
    nq = seq // tile
    n_steps = batch * N_HEADS * nq
    ra, rb = w_a.shape[0] // n_steps, w_b.shape[0] // n_steps
    assert ra * n_steps == w_a.shape[0] and rb * n_steps == w_b.shape[0]
    kern = functools.partial(_prompt_attn_kernel, tile=tile, out_scale=out_scale)
    qmap = lambda b, h, i: (b * nq + i, h)
    kvmap = lambda b, h, i: (b, h)
    tmap = lambda b, h, i: (h, 0, 0)
    wmap = lambda b, h, i: ((b * N_HEADS + h) * nq + i, 0)
    wa_spec = pl.BlockSpec((ra, w_a.shape[1]), wmap)
    wb_spec = pl.BlockSpec((rb, w_b.shape[1]), wmap)
    return pl.pallas_call(
        kern,
        grid=(batch, N_HEADS, nq),
        in_specs=[pl.BlockSpec(memory_space=pltpu.SMEM),
                  pl.BlockSpec((tile, HEAD_W), qmap),
                  pl.BlockSpec((seq, HEAD_W), kvmap),
                  pl.BlockSpec((seq, HEAD_W), kvmap),
                  pl.BlockSpec((1, tile, tile), tmap),
                  pl.BlockSpec((1, tile, tile), tmap),
                  pl.BlockSpec((1, HEAD_W), lambda b, h, i: (0, 0)),
                  wa_spec, wb_spec],
        out_specs=[pl.BlockSpec((tile, HEAD_W), qmap), wa_spec, wb_spec],
        out_shape=[jax.ShapeDtypeStruct((batch * seq, ATTN_WIDTH), BF16),
                   jax.ShapeDtypeStruct(w_a.shape, BF16), jax.ShapeDtypeStruct(w_b.shape, BF16)],
        scratch_shapes=[pltpu.VMEM((nq, HEAD_W, tile), BF16),
                        pltpu.VMEM((1, 2 * tile), F32), pltpu.VMEM((1, 2 * tile), F32),
                        pltpu.VMEM((HEAD_W, 2 * tile), F32)],
        compiler_params=_cparams("parallel", "parallel", "arbitrary"),
        name="prompt_attn",
    )(lam, q, k_bf, v_bf, t0, t1, subln_g, w_a, w_b)


def _softmax_update(s_list, v_list, m_sc, l_sc, acc_sc):
    m_prev = m_sc[...]
    m_new = m_prev
    for s in s_list:
        m_new = jnp.maximum(m_new, jnp.max(s, axis=-1, keepdims=True))
    alpha = jnp.exp(m_prev - m_new)
    l_new = alpha * l_sc[...]
    acc = alpha * acc_sc[...]
    for s, vj in zip(s_list, v_list):
        p = jnp.exp(s - m_new)
        l_new = l_new + jnp.sum(p, axis=-1, keepdims=True)
        acc = acc + jnp.dot(p.astype(BF16), vj, preferred_element_type=F32)
    l_sc[...] = l_new
    acc_sc[...] = acc
    m_sc[...] = m_new


def _sample_attn_kernel(pt_ref, lam_ref, q_ref, *refs, n_pp, out_scale):
    k_refs = refs[:n_pp]
    v_refs = refs[n_pp:2 * n_pp]
    tab_ref, ks_ref, vs_ref, tabs_ref, g_ref, o_ref, m_sc, l_sc, acc_sc = refs[2 * n_pp:]
    s_idx = pl.program_id(1)
    cols = k_refs[0].shape[1]

    @pl.when(s_idx == 0)
    def _():
        m_sc[...] = jnp.full(m_sc.shape, NEG_INF, F32)
        l_sc[...] = jnp.zeros(l_sc.shape, F32)
        acc_sc[...] = jnp.zeros(acc_sc.shape, F32)

    q2 = _split_maps(q_ref[0])
    s_list = [_qk(q2, k_refs[pp][0].astype(BF16)) + tab_ref[0, :, pp * cols:(pp + 1) * cols]
              for pp in range(n_pp)]
    _softmax_update(s_list, [v_refs[pp][0].astype(BF16) for pp in range(n_pp)], m_sc, l_sc, acc_sc)

    @pl.when(s_idx == pl.num_programs(1) - 1)
    def _():
        s = _qk(q2, ks_ref[0].astype(BF16)) + tabs_ref[...]
        _softmax_update([s], [vs_ref[0].astype(BF16)], m_sc, l_sc, acc_sc)
        o = acc_sc[...] / l_sc[...]
        r = o.shape[0] // 2
        d = o[:r] - lam_ref[0] * o[r:]
        o_ref[0] = (_rms(d, g_ref[...]) * out_scale).astype(o_ref.dtype)


def _sample_attention(q3, cache_k3, cache_v3, page_table, tab, k_self, v_self, tab_self, lam, subln_g,
                      out_scale):
    nb, rows, _ = q3.shape
    n_pages = page_table.shape[1]
    n_pp = PAGES_PER_STEP
    n_steps = n_pages // n_pp
    cols = cache_k3.shape[1]
    kern = functools.partial(_sample_attn_kernel, n_pp=n_pp, out_scale=out_scale)

    def page_map(pp):
        return lambda b, s, pt: (pt[b, s * n_pp + pp], 0, 0)

    page_specs = [pl.BlockSpec((1, cols, HEAD_W), page_map(pp)) for pp in range(n_pp)]
    bmap = lambda b, s, pt: (b, 0, 0)
    grid_spec = pltpu.PrefetchScalarGridSpec(
        num_scalar_prefetch=1,
        grid=(nb, n_steps),
        in_specs=[pl.BlockSpec(memory_space=pltpu.SMEM),
                  pl.BlockSpec((1, rows, HEAD_W), bmap)]
                 + page_specs + page_specs
                 + [pl.BlockSpec((1, 2 * rows, n_pp * cols),
                                 lambda b, s, pt: (jnp.where(s == n_steps - 1, 1, 0), 0, 0)),
                    pl.BlockSpec((1, HEAD_W, HEAD_W), bmap),
                    pl.BlockSpec((1, HEAD_W, HEAD_W), bmap),
                    pl.BlockSpec((2 * rows, HEAD_W), lambda b, s, pt: (0, 0)),
                    pl.BlockSpec((1, HEAD_W), lambda b, s, pt: (0, 0))],
        out_specs=pl.BlockSpec((1, rows, HEAD_W), bmap),
        scratch_shapes=[pltpu.VMEM((2 * rows, 1), F32), pltpu.VMEM((2 * rows, 1), F32),
                        pltpu.VMEM((2 * rows, HEAD_W), F32)],
    )
    return pl.pallas_call(
        kern,
        grid_spec=grid_spec,
        out_shape=jax.ShapeDtypeStruct((nb, rows, HEAD_W), BF16),
        compiler_params=_cparams("parallel", "arbitrary"),
        name="sample_attn",
    )(page_table, lam, q3, *([cache_k3] * n_pp), *([cache_v3] * n_pp), tab, k_self, v_self, tab_self,
      subln_g)


def _softplus(x):
    return jnp.maximum(x, 0.0) + jnp.log1p(jnp.exp(-jnp.abs(x)))


def _gelu_tanh(x):
    return 0.5 * x * (1.0 + jnp.tanh(math.sqrt(2.0 / math.pi) * (x + 0.044715 * x * x * x)))


def _block_gate(cb, w_ref, b):
    parts = [jnp.dot(cb[:, n * LRU_BLOCK_W:(n + 1) * LRU_BLOCK_W], w_ref[n], preferred_element_type=F32)
             for n in range(LRU_BLOCKS)]
    return jax.nn.sigmoid(jnp.concatenate(parts, axis=-1) + b)


def _lru_coeffs(conv, wrg_ref, brg, wig_ref, big, lam):
    cb = conv.astype(BF16)
    rg = _block_gate(cb, wrg_ref, brg)
    ig = _block_gate(cb, wig_ref, big)
    log_a = -LRU_C * rg * _softplus(-lam)
    a = jnp.exp(log_a)
    bx = jnp.sqrt(jnp.tanh(-log_a) * (1.0 + a * a)) * ig * conv
    return a, bx


def _prompt_lru_kernel(xr_ref, gate_ref, cw_ref, cb_ref, wrg_ref, brg_ref, wig_ref, big_ref, lam_ref,
                       y_ref, conv_ref, h_ref, xc_sc, h_sc, *, tile):
    t = pl.program_id(1)
    pad = SUBLANES

    @pl.when(t == 0)
    def _():
        xc_sc[0:pad, :] = jnp.zeros((pad, LRU_WIDTH), F32)
        h_sc[...] = jnp.zeros(h_sc.shape, F32)

    xr = xr_ref[...]
    xc_sc[pad:pad + tile, :] = xr
    conv = cb_ref[...]
    for j in range(CONV_W):
        off = pad - (CONV_W - 1) + j
        conv = conv + xc_sc[off:off + tile, :] * cw_ref[j:j + 1, :]
    xc_sc[0:pad, :] = xr[tile - pad:, :]

    a, bx = _lru_coeffs(conv, wrg_ref, brg_ref[...], wig_ref, big_ref[...], lam_ref[...])

    row = lax.broadcasted_iota(I32, a.shape, 0)
    d = 1
    while d < tile:
        keep = row >= d
        a_sh = pltpu.roll(a, d, 0)
        b_sh = pltpu.roll(bx, d, 0)
        bx = jnp.where(keep, a * b_sh + bx, bx)
        a = jnp.where(keep, a * a_sh, a)
        d *= 2
    h = bx + a * h_sc[...]
    h_sc[...] = h[tile - 1:tile, :]
    y_ref[...] = (h * _gelu_tanh(gate_ref[...])).astype(y_ref.dtype)

    @pl.when(t == pl.num_programs(1) - 1)
    def _():
        conv_ref[0] = xc_sc[pad + tile - (CONV_W - 1):pad + tile, :]
        h_ref[0] = h[tile - 1:tile, :]


def _prompt_lru(xr, gate, conv_w, conv_b, w_rg, b_rg, w_ig, b_ig, lru_lambda, batch, seq):
    tile = LRU_TILE
    nt = seq // tile
    kern = functools.partial(_prompt_lru_kernel, tile=tile)
    rmap = lambda b, t: (b * nt + t, 0)
    full2 = lambda b, t: (0, 0)
    full3 = lambda b, t: (0, 0, 0)
    bmap = lambda b, t: (b, 0, 0)
    vec = pl.BlockSpec((1, LRU_WIDTH), full2)
    wspec = pl.BlockSpec((LRU_BLOCKS, LRU_BLOCK_W, LRU_BLOCK_W), full3)
    return pl.pallas_call(
        kern,
        grid=(batch, nt),
        in_specs=[pl.BlockSpec((tile, LRU_WIDTH), rmap), pl.BlockSpec((tile, LRU_WIDTH), rmap),
                  pl.BlockSpec((CONV_W, LRU_WIDTH), full2), vec, wspec, vec, wspec, vec, vec],
        out_specs=[pl.BlockSpec((tile, LRU_WIDTH), rmap),
                   pl.BlockSpec((1, CONV_W - 1, LRU_WIDTH), bmap),
                   pl.BlockSpec((1, 1, LRU_WIDTH), bmap)],
        out_shape=[jax.ShapeDtypeStruct((batch * seq, LRU_WIDTH), BF16),
                   jax.ShapeDtypeStruct((batch, CONV_W - 1, LRU_WIDTH), F32),
                   jax.ShapeDtypeStruct((batch, 1, LRU_WIDTH), F32)],
        scratch_shapes=[pltpu.VMEM((tile + SUBLANES, LRU_WIDTH), F32), pltpu.VMEM((1, LRU_WIDTH), F32)],
        compiler_params=_cparams("parallel", "arbitrary"),
        name="prompt_lru",
    )(xr, gate, conv_w, conv_b, w_rg, b_rg, w_ig, b_ig, lru_lambda)


def _sample_lru_kernel(xr_ref, gate_ref, cprev_ref, hprev_ref, cw_ref, cb_ref, wrg_ref, brg_ref,
                       wig_ref, big_ref, lam_ref, y_ref, conv_ref, h_ref, *, steps):
    xcat = [cprev_ref[j] for j in range(CONV_W - 1)] + [xr_ref[t] for t in range(steps)]
    h = hprev_ref[...]
    for t in range(steps):
        conv = cb_ref[...]
        for j in range(CONV_W):
            conv = conv + xcat[t + j] * cw_ref[j:j + 1, :]
        a, bx = _lru_coeffs(conv, wrg_ref, brg_ref[...], wig_ref, big_ref[...], lam_ref[...])
        h = a * h + bx
        y_ref[t] = (h * _gelu_tanh(gate_ref[t])).astype(y_ref.dtype)
    for j in range(CONV_W - 1):
        conv_ref[j] = xcat[steps + j]
    h_ref[...] = h


def _sample_lru(xr_t, gate_t, cprev_t, hprev, conv_w, conv_b, w_rg, b_rg, w_ig, b_ig, lru_lambda):
    steps, nb, _ = xr_t.shape
    kern = functools.partial(_sample_lru_kernel, steps=steps)
    return pl.pallas_call(
        kern,
        out_shape=[jax.ShapeDtypeStruct((steps, nb, LRU_WIDTH), BF16),
                   jax.ShapeDtypeStruct((CONV_W - 1, nb, LRU_WIDTH), F32),
                   jax.ShapeDtypeStruct((nb, LRU_WIDTH), F32)],
        compiler_params=pltpu.CompilerParams(vmem_limit_bytes=VMEM_LIMIT),
        name="sample_lru",
    )(xr_t, gate_t, cprev_t, hprev, conv_w, conv_b, w_rg, b_rg, w_ig, b_ig, lru_lambda)


def _split_bf16(x):
    hi = x.astype(BF16)
    return hi, (x - hi.astype(F32)).astype(BF16)


def _outproj_kernel(a_ref, l_ref, w_ref, x_ref, g_ref, rw_ref, rb_ref, xmid_ref, xp_ref, logit_ref):
    tm = x_ref.shape[0]
    y = jnp.dot(a_ref[...], w_ref[0:ATTN_WIDTH, :], preferred_element_type=F32)
    y = y + jnp.dot(l_ref[...], w_ref[ATTN_WIDTH:, :], preferred_element_type=F32)
    xm = x_ref[...] + y
    xmid_ref[...] = xm
    xn = _rms(xm, g_ref[...])
    xn_hi, xn_lo = _split_bf16(xn)
    bits = _bf16_bits(xn)
    words = bits[:, PACK_W:] | (bits[:, :PACK_W] >> 16)
    xp_ref[...] = words.reshape(tm, PACK_S, LANES)
    rw_hi, rw_lo = _split_bf16(rw_ref[...])
    logits = (jnp.dot(xn_hi, rw_hi, preferred_element_type=F32)
              + jnp.dot(xn_lo, rw_hi, preferred_element_type=F32)
              + jnp.dot(xn_hi, rw_lo, preferred_element_type=F32))
    logit_ref[...] = logits + rb_ref[...]


def _outproj(attn, lru, w_out_bf, x2, g, router_w, router_b, tm):
    n = x2.shape[0]
    row = lambda i: (i, 0)
    full = lambda i: (0, 0)
    return pl.pallas_call(
        _outproj_kernel,
        grid=(n // tm,),
        in_specs=[pl.BlockSpec((tm, ATTN_WIDTH), row), pl.BlockSpec((tm, LRU_WIDTH), row),
                  pl.BlockSpec((D_MODEL, D_MODEL), full, pipeline_mode=pl.Buffered(1)),
                  pl.BlockSpec((tm, D_MODEL), row),
                  pl.BlockSpec((1, D_MODEL), full), pl.BlockSpec((D_MODEL, N_EXPERTS), full),
                  pl.BlockSpec((1, N_EXPERTS), full)],
        out_specs=[pl.BlockSpec((tm, D_MODEL), row), pl.BlockSpec((tm, PACK_S, LANES), lambda i: (i, 0, 0)),
                   pl.BlockSpec((tm, N_EXPERTS), row)],
        out_shape=[jax.ShapeDtypeStruct((n, D_MODEL), F32), jax.ShapeDtypeStruct((n, PACK_S, LANES), U32),
                   jax.ShapeDtypeStruct((n, N_EXPERTS), F32)],
        compiler_params=_cparams("parallel"),
        name="outproj",
    )(attn, lru, w_out_bf, x2, g, router_w, router_b)


def _route_kernel(logit_ref, base_ref, idx_ref, gate_ref, rank_ref, cnt_ref, cnt_sc):
    i = pl.program_id(0)
    tr = logit_ref.shape[0]

    @pl.when(i == 0)
    def _():
        cnt_sc[...] = base_ref[...]

    logits = logit_ref[...]
    lane = lax.broadcasted_iota(I32, logits.shape, 1)
    vals, firsts, hots = [], [], []
    for _ in range(TOP_K):
        mx = jnp.max(logits, axis=-1, keepdims=True)
        first = jnp.min(jnp.where(logits == mx, lane, N_EXPERTS), axis=-1, keepdims=True)
        hot = lane == first
        vals.append(mx)
        firsts.append(first)
        hots.append(hot)
        logits = jnp.where(hot, -jnp.inf, logits)

    member = jnp.zeros(logits.shape, F32)
    for hot in hots:
        member = member + hot.astype(F32)
    r_i = lax.broadcasted_iota(I32, (tr, tr), 0)
    c_i = lax.broadcasted_iota(I32, (tr, tr), 1)
    earlier = jnp.where(c_i < r_i, 1.0, 0.0).astype(BF16)
    before = cnt_sc[...] + jnp.dot(earlier, member.astype(BF16), preferred_element_type=F32)

    exps = [jnp.exp(v - vals[0]) for v in vals]
    denom = exps[0]
    for e in exps[1:]:
        denom = denom + e
    for k in range(TOP_K):
        idx_ref[:, k:k + 1] = firsts[k]
        gate_ref[:, k:k + 1] = exps[k] / denom
        rank = jnp.sum(jnp.where(hots[k], before, 0.0), axis=-1, keepdims=True)
        rank_ref[:, k:k + 1] = rank.astype(I32)
    cnt_sc[...] = cnt_sc[...] + jnp.sum(member, axis=0, keepdims=True)

    @pl.when(i == pl.num_programs(0) - 1)
    def _():
        cnt_ref[...] = cnt_sc[...]


def _route(logits, base_counts, tr):
    n = logits.shape[0]
    row = lambda i: (i, 0)
    full = lambda i: (0, 0)
    k_spec = pl.BlockSpec((tr, TOP_K), row)
    return pl.pallas_call(
        _route_kernel,
        grid=(n // tr,),
        in_specs=[pl.BlockSpec((tr, N_EXPERTS), row), pl.BlockSpec((1, N_EXPERTS), full)],
        out_specs=[k_spec, k_spec, k_spec, pl.BlockSpec((1, N_EXPERTS), full)],
        out_shape=[jax.ShapeDtypeStruct((n, TOP_K), I32), jax.ShapeDtypeStruct((n, TOP_K), F32),
                   jax.ShapeDtypeStruct((n, TOP_K), I32), jax.ShapeDtypeStruct((1, N_EXPERTS), F32)],
        scratch_shapes=[pltpu.VMEM((1, N_EXPERTS), F32)],
        compiler_params=_cparams("arbitrary"),
        name="route",
    )(logits, base_counts)


def _dispatch_kernel(pos_ref, x_ref, xs_in_ref, xs_ref, sem):
    del xs_in_ref
    tm = x_ref.shape[0]

    def issue(r, c):
        for k in range(TOP_K):
            pltpu.make_async_copy(x_ref.at[r], xs_ref.at[pos_ref[r * TOP_K + k]], sem).start()
        return c

    lax.fori_loop(0, tm, issue, 0)
    for _ in range(TOP_K):
        pltpu.make_async_copy(x_ref, xs_ref.at[pl.ds(0, tm)], sem).wait()


def _dispatch(pos_flat, x_packed, x_sorted, tm):
    n = x_packed.shape[0]
    return pl.pallas_call(
        _dispatch_kernel,
        grid=(n // tm,),
        in_specs=[pl.BlockSpec((tm * TOP_K,), lambda i: (i,), memory_space=pltpu.SMEM),
                  pl.BlockSpec((tm, PACK_S, LANES), lambda i: (i, 0, 0)),
                  pl.BlockSpec(memory_space=pl.ANY)],
        out_specs=pl.BlockSpec(memory_space=pl.ANY),
        out_shape=jax.ShapeDtypeStruct(x_sorted.shape, x_sorted.dtype),
        scratch_shapes=[pltpu.SemaphoreType.DMA],
        input_output_aliases={2: 0},
        compiler_params=_cparams("arbitrary"),
        name="moe_dispatch",
    )(pos_flat, x_packed, x_sorted)


def _moe_kernel(te_ref, na_ref, x_ref, wg_ref, wu_ref, wd_ref, bg_ref, bu_ref, bd_ref, o_ref,
                xb_sc, act_sc, lo_sc, *, n_f):
    i = pl.program_id(0)
    j = pl.program_id(1)
    tm = x_ref.shape[0]
    tf = act_sc.shape[2]
    last = pl.num_programs(1) - 1
    active = i < na_ref[0]

    @pl.when(jnp.logical_and(active, j == 0))
    def _():
        lo, hi = _unpack_words(x_ref[...].reshape(tm, PACK_W))
        xb_sc[:, :PACK_W] = lo.astype(BF16)
        xb_sc[:, PACK_W:] = hi.astype(BF16)

    @pl.when(jnp.logical_and(active, j < n_f))
    def _():
        x = xb_sc[...]
        glu = jnp.dot(x, wg_ref[0], preferred_element_type=F32) + bg_ref[0]
        lin = jnp.dot(x, wu_ref[0], preferred_element_type=F32) + bu_ref[0]
        glu = jnp.minimum(glu, SWIGLU_LIMIT)
        lin = jnp.clip(lin, -SWIGLU_LIMIT, SWIGLU_LIMIT)
        act_sc[j] = (glu * jax.nn.sigmoid(SWIGLU_ALPHA * glu) * (lin + 1.0)).astype(BF16)

    @pl.when(jnp.logical_and(active, j >= n_f))
    def _():
        y = bd_ref[0] + jnp.dot(act_sc[0], wd_ref[0, 0:tf, :], preferred_element_type=F32)
        for f in range(1, n_f):
            y = y + jnp.dot(act_sc[f], wd_ref[0, f * tf:(f + 1) * tf, :], preferred_element_type=F32)
        bits = _bf16_bits(y)

        @pl.when(j == n_f)
        def _():
            lo_sc[...] = bits >> 16

        @pl.when(j == last)
        def _():
            o_ref[...] = (bits | lo_sc[...]).reshape(tm, PACK_S, LANES)

    @pl.when(jnp.logical_and(jnp.logical_not(active), j == last))
    def _():
        o_ref[...] = jnp.zeros(o_ref.shape, U32)


def _moe_ffn(x_sorted, tile_expert, n_active, w_gate_up, b_gate_up, w_down, b_down):
    n_slots = x_sorted.shape[0]
    tm, tf = MOE_TILE, MOE_FF_TILE
    n_tiles = n_slots // tm
    n_f = D_FF // tf
    n_d = D_MODEL // PACK_W
    kern = functools.partial(_moe_kernel, n_f=n_f)

    def fidx(i, j, na):
        return jnp.where(i < na[0], jnp.minimum(j, n_f - 1), n_f - 1)

    def didx(i, j, na):
        return jnp.where(i < na[0], jnp.clip(j - n_f, 0, n_d - 1), n_d - 1)

    def rowidx(i, na):
        return jnp.minimum(i, na[0] - 1)

    grid_spec = pltpu.PrefetchScalarGridSpec(
        num_scalar_prefetch=2,
        grid=(n_tiles, n_f + n_d),
        in_specs=[pl.BlockSpec((tm, PACK_S, LANES), lambda i, j, te, na: (rowidx(i, na), 0, 0)),
                  pl.BlockSpec((1, D_MODEL, tf), lambda i, j, te, na: (te[i], 0, fidx(i, j, na))),
                  pl.BlockSpec((1, D_MODEL, tf), lambda i, j, te, na: (te[i], 0, n_f + fidx(i, j, na))),
                  pl.BlockSpec((1, D_FF, PACK_W), lambda i, j, te, na: (te[i], 0, didx(i, j, na))),
                  pl.BlockSpec((1, 1, tf), lambda i, j, te, na: (te[i], 0, fidx(i, j, na))),
                  pl.BlockSpec((1, 1, tf), lambda i, j, te, na: (te[i], 0, n_f + fidx(i, j, na))),
                  pl.BlockSpec((1, 1, PACK_W), lambda i, j, te, na: (te[i], 0, didx(i, j, na)))],
        out_specs=pl.BlockSpec((tm, PACK_S, LANES), lambda i, j, te, na: (i, 0, 0)),
        scratch_shapes=[pltpu.VMEM((tm, D_MODEL), BF16), pltpu.VMEM((n_f, tm, tf), BF16),
                        pltpu.VMEM((tm, PACK_W), U32)],
    )
    bgu3 = b_gate_up.reshape(N_EXPERTS, 1, 2 * D_FF)
    bd3 = b_down.reshape(N_EXPERTS, 1, D_MODEL)
    return pl.pallas_call(
        kern,
        grid_spec=grid_spec,
        out_shape=jax.ShapeDtypeStruct((n_slots, PACK_S, LANES), U32),
        compiler_params=_cparams("arbitrary", "arbitrary"),
        name="moe_ffn",
    )(tile_expert, n_active, x_sorted, w_gate_up, w_gate_up, w_down, bgu3, bgu3, bd3)


def _combine_kernel(pos_ref, gate_ref, x_ref, g_ref, y_hbm, o_ref, ybuf, sem):
    tm = x_ref.shape[0]

    def issue(r, c):
        for k in range(TOP_K):
            pltpu.make_async_copy(y_hbm.at[pos_ref[r * TOP_K + k]], ybuf.at[k, r], sem).start()
        return c

    lax.fori_loop(0, tm, issue, 0)
    for k in range(TOP_K):
        pltpu.make_async_copy(y_hbm.at[pl.ds(0, tm)], ybuf.at[k], sem).wait()

    gates = gate_ref[...]
    x = x_ref[...]
    out_lo, out_hi = x[:, :PACK_W], x[:, PACK_W:]
    for k in range(TOP_K):
        lo, hi = _unpack_words(ybuf[k].reshape(tm, PACK_W))
        out_lo = out_lo + gates[:, k:k + 1] * lo
        out_hi = out_hi + gates[:, k:k + 1] * hi
    o_ref[...] = _rms(jnp.concatenate([out_lo, out_hi], axis=1), g_ref[...])


def _combine(pos_flat, gates, x_mid, g, y_sorted, tm):
    n = x_mid.shape[0]
    row = lambda i: (i, 0)
    return pl.pallas_call(
        _combine_kernel,
        grid=(n // tm,),
        in_specs=[pl.BlockSpec((tm * TOP_K,), lambda i: (i,), memory_space=pltpu.SMEM),
                  pl.BlockSpec((tm, TOP_K), row),
                  pl.BlockSpec((tm, D_MODEL), row),
                  pl.BlockSpec((1, D_MODEL), lambda i: (0, 0)),
                  pl.BlockSpec(memory_space=pl.ANY)],
        out_specs=pl.BlockSpec((tm, D_MODEL), row),
        out_shape=jax.ShapeDtypeStruct((n, D_MODEL), F32),
        scratch_shapes=[pltpu.VMEM((TOP_K, tm, PACK_S, LANES), U32), pltpu.SemaphoreType.DMA],
        compiler_params=_cparams("arbitrary"),
        name="moe_combine",
    )(pos_flat, gates, x_mid, g, y_sorted)


def _bias_of_distance(rel_table, n):
    nf = jnp.maximum(n, 1).astype(F32)
    large = REL_EXACT + (jnp.log(nf / REL_EXACT) / math.log(REL_MAX_DIST / REL_EXACT)
                         * (REL_BUCKETS - REL_EXACT)).astype(I32)
    bucket = jnp.where(n < REL_EXACT, n, jnp.minimum(large, REL_BUCKETS - 1))
    hot = bucket[..., None, None] == jnp.arange(REL_BUCKETS, dtype=I32)[:, None]
    return jnp.sum(jnp.where(hot, rel_table.astype(F32), 0.0), axis=-2)


def _prompt_bias_tables(rel_table, tile):
    far = rel_table[REL_BUCKETS - 1].astype(F32)
    d = jnp.arange(tile)[None, :] - jnp.arange(tile)[:, None]
    diag = jnp.where((d >= 0)[..., None], _bias_of_distance(rel_table, jnp.maximum(d, 0)) - far, NEG_INF)
    near = _bias_of_distance(rel_table, d + tile) - far
    return jnp.transpose(diag, (2, 0, 1)), jnp.transpose(near, (2, 0, 1))


def _sample_bias_tables(rel_table, dec_seq, past_len, page_size):
    heads = jnp.arange(N_HEADS)
    same_head = heads[:, None] == heads[None, :]

    def table(dist, valid):
        b = _bias_of_distance(rel_table, jnp.maximum(dist, 0))
        b = jnp.transpose(b, (0, 2, 1))
        full = jnp.where(same_head[None, :, None, :] & valid[:, None, :, None],
                         b[:, :, :, None], NEG_INF)
        full = full.reshape(dec_seq * N_HEADS, -1)
        return jnp.concatenate([full, full], axis=0)

    qpos = past_len + jnp.arange(dec_seq)
    span = PAGES_PER_STEP * page_size
    always = jnp.ones((dec_seq, span), bool)
    k_last = past_len - span + jnp.arange(span)
    tab_far = table(jnp.full((dec_seq, span), REL_MAX_DIST), always)
    tab_last = table(qpos[:, None] - k_last[None, :], always)
    n_self = HEAD_W // N_HEADS
    tk = jnp.arange(n_self)
    dist = jnp.arange(dec_seq)[:, None] - tk[None, :]
    tab_self = table(dist, (dist >= 0) & (tk[None, :] < dec_seq))
    return jnp.stack([tab_far, tab_last]), tab_self


def _tile_plan(counts, n_assign, tm):
    counts = counts.reshape(N_EXPERTS).astype(I32)
    padded = (counts + tm - 1) // tm * tm
    pad_end = jnp.cumsum(padded)
    pad_start = pad_end - padded
    n_tiles = n_assign // tm + N_EXPERTS
    n_active = (pad_end[-1] // tm).astype(I32)
    tile_start = jnp.minimum(jnp.arange(n_tiles, dtype=I32), n_active - 1) * tm
    tile_expert = jnp.minimum(jnp.sum((pad_end[None, :] <= tile_start[:, None]).astype(I32), axis=1),
                              N_EXPERTS - 1)
    return pad_start, tile_expert, n_active.reshape(1), n_tiles


def _slots(pad_start, idx, rank):
    hot = idx[..., None] == jnp.arange(N_EXPERTS, dtype=I32)
    return (jnp.sum(jnp.where(hot, pad_start, 0), axis=-1) + rank).reshape(-1)


def kernel(x_prompt, x_sample, cache_k, cache_v, state_conv, state_h, page_table, rel_bias, norm_mix_g, w_in, lam_q1, lam_k1, lam_q2, lam_k2, subln_g, conv_w, conv_b, w_rg, b_rg, w_ig, b_ig, lru_lambda, w_out, norm_ffn_g, router_w, router_b, w_gate_up, b_gate_up, w_down, b_down, norm_final_g):
    depth = w_in.shape[0]
    assert depth == 1
    layer = 0
    bp, seq, _ = x_prompt.shape
    bs, dec_seq, _ = x_sample.shape
    n_pool, page_size = cache_k.shape[1], cache_k.shape[2]
    past_len = page_table.shape[1] * page_size
    n_p, n_s = bp * seq, bs * dec_seq
    assert ATTN_TILE >= REL_MAX_DIST and PAGES_PER_STEP * page_size >= REL_MAX_DIST

    lam_init = 0.8 - 0.6 * math.exp(-0.3 * layer)
    out_scale = 1.0 - lam_init
    lam = (jnp.exp(jnp.sum(lam_q1[layer] * lam_k1[layer])) - jnp.exp(jnp.sum(lam_q2[layer] * lam_k2[layer]))
           + lam_init).reshape(1).astype(F32)

    g_mix = norm_mix_g[layer].reshape(1, D_MODEL)
    g_ffn = norm_ffn_g[layer].reshape(1, D_MODEL)
    g_fin = norm_final_g.reshape(1, D_MODEL)
    g_sub = subln_g[layer].reshape(1, HEAD_W)
    w_in_bf = w_in[layer].astype(BF16)
    w_out_bf = w_out[layer].astype(BF16)
    lru_args = (conv_w[layer], conv_b[layer].reshape(1, LRU_WIDTH),
                w_rg[layer].astype(BF16), b_rg[layer].reshape(1, LRU_WIDTH),
                w_ig[layer].astype(BF16), b_ig[layer].reshape(1, LRU_WIDTH),
                lru_lambda[layer].reshape(1, LRU_WIDTH))
    rw = router_w[layer]
    rb = router_b[layer].reshape(1, N_EXPERTS)

    xp2 = x_prompt.reshape(n_p, D_MODEL)
    xs2 = x_sample.reshape(n_s, D_MODEL)

    q_p, k_p, v_p, kb_p, vb_p, xr_p, gate_p = _inproj(xp2, g_mix, w_in_bf, ROW_TILE)
    t0, t1 = _prompt_bias_tables(rel_bias, ATTN_TILE)
    attn_p, wgu_bf, wd_bf = _prompt_attention(
        q_p, kb_p, vb_p, t0, t1, lam, g_sub,
        w_gate_up[layer].reshape(N_EXPERTS * D_MODEL, 2 * D_FF), w_down[layer].reshape(N_EXPERTS * D_FF, D_MODEL),
        bp, seq, out_scale)
    lru_p, conv_p, h_p = _prompt_lru(xr_p, gate_p, *lru_args, bp, seq)
    xmid_p, xpk_p, logit_p = _outproj(attn_p, lru_p, w_out_bf, xp2, g_ffn, rw, rb, ROW_TILE)

    q_s, k_s, v_s, _, _, xr_s, gate_s = _inproj(xs2, g_mix, w_in_bf, n_s)
    tab, tab_self = _sample_bias_tables(rel_bias, dec_seq, past_len, page_size)
    rows = dec_seq * N_HEADS
    pad_rows = HEAD_W - rows

    def self_rows(a):
        a3 = a.reshape(bs, rows, HEAD_W)
        return jnp.concatenate([a3, jnp.zeros((bs, pad_rows, HEAD_W), a.dtype)], axis=1)

    attn_s = _sample_attention(
        q_s.reshape(bs, rows, HEAD_W),
        cache_k[layer].reshape(n_pool, page_size * N_HEADS, HEAD_W),
        cache_v[layer].reshape(n_pool, page_size * N_HEADS, HEAD_W),
        page_table, tab, self_rows(k_s), self_rows(v_s), tab_self, lam, g_sub, out_scale)
    attn_s = attn_s.reshape(n_s, ATTN_WIDTH)

    def tmajor(a):
        return jnp.transpose(a.reshape(bs, dec_seq, LRU_WIDTH), (1, 0, 2))

    lru_s_t, conv_s_t, h_s = _sample_lru(tmajor(xr_s), tmajor(gate_s),
                                         jnp.transpose(state_conv[layer], (1, 0, 2)), state_h[layer],
                                         *lru_args)
    lru_s = jnp.transpose(lru_s_t, (1, 0, 2)).reshape(n_s, LRU_WIDTH)
    conv_s = jnp.transpose(conv_s_t, (1, 0, 2))
    xmid_s, xpk_s, logit_s = _outproj(attn_s, lru_s, w_out_bf, xs2, g_ffn, rw, rb, n_s)

    idx_p, gates_p, rank_p, cnt_p = _route(logit_p, jnp.zeros((1, N_EXPERTS), F32), ROW_TILE)
    idx_s, gates_s, rank_s, cnt = _route(logit_s, cnt_p, n_s)
    n_assign = (n_p + n_s) * TOP_K
    pad_start, tile_expert, n_active, n_tiles = _tile_plan(cnt, n_assign, MOE_TILE)
    pos_p = _slots(pad_start, idx_p, rank_p)
    pos_s = _slots(pad_start, idx_s, rank_s)
    x_sorted = jnp.zeros((n_tiles * MOE_TILE, PACK_S, LANES), U32)
    x_sorted = _dispatch(pos_p, xpk_p, x_sorted, ROW_TILE)
    x_sorted = _dispatch(pos_s, xpk_s, x_sorted, n_s)
    y_sorted = _moe_ffn(x_sorted, tile_expert, n_active, wgu_bf.reshape(N_EXPERTS, D_MODEL, 2 * D_FF),
                        b_gate_up[layer], wd_bf.reshape(N_EXPERTS, D_FF, D_MODEL), b_down[layer])
    y_p = _combine(pos_p, gates_p, xmid_p, g_fin, y_sorted, COMBINE_TILE)
    y_s = _combine(pos_s, gates_s, xmid_s, g_fin, y_sorted, n_s)

    kv_p = (depth, bp, seq, N_HEADS, HEAD_W)
    kv_s = (depth, bs, dec_seq, N_HEADS, HEAD_W)
    return (y_p.reshape(bp, seq, D_MODEL), y_s.reshape(bs, dec_seq, D_MODEL),
            k_p.reshape(kv_p), v_p.reshape(kv_p),
            conv_p.reshape(depth, bp, CONV_W - 1, LRU_WIDTH), h_p.reshape(depth, bp, LRU_WIDTH),
            k_s.reshape(kv_s), v_s.reshape(kv_s),
            conv_s.reshape(depth, bs, CONV_W - 1, LRU_WIDTH), h_s.reshape(depth, bs, LRU_WIDTH))
```

```python
import functools
import math

import jax
import jax.numpy as jnp
from jax import lax
from jax.experimental import pallas as pl
from jax.experimental.pallas import tpu as pltpu

F32 = jnp.float32
BF16 = jnp.bfloat16
U32 = jnp.uint32
I32 = jnp.int32

D_MODEL = 2048
ATTN_WIDTH = D_MODEL // 2
LRU_WIDTH = D_MODEL - ATTN_WIDTH
HEAD_DIM = 64
HEAD_W = 2 * HEAD_DIM
N_HEADS = ATTN_WIDTH // HEAD_W
LRU_BLOCKS = 8
LRU_BLOCK_W = LRU_WIDTH // LRU_BLOCKS
CONV_W = 4
LRU_C = 8.0
REL_BUCKETS = 32
REL_EXACT = REL_BUCKETS // 2
REL_MAX_DIST = 128
N_EXPERTS = 32
TOP_K = 4
D_FF = D_MODEL
SWIGLU_LIMIT = 7.0
SWIGLU_ALPHA = 1.702
EPS = 1e-5
NEG_INF = -1e30
N_PROJ = 5
PROJ_W = 1024

LANES = 128
SUBLANES = 8
VMEM_LIMIT = 56 * 1024 * 1024
PACK_W = D_MODEL // 2
PACK_S = PACK_W // LANES
ROW_S = D_MODEL // LANES

ATTN_TILE = 512
LRU_TILE = 256
ROW_TILE = 512
MOE_TILE = 512
MOE_FF_TILE = 512
COMBINE_TILE = 512
COMBINE_GROUP = 32
PAGES_PER_STEP = 8


def _cparams(*sem):
    return pltpu.CompilerParams(dimension_semantics=sem, vmem_limit_bytes=VMEM_LIMIT)


def _rms(x, g):
    return x * lax.rsqrt(jnp.mean(x * x, axis=-1, keepdims=True) + EPS) * g


def _bf16_bits(x):
    return pltpu.bitcast(x.astype(BF16).astype(F32), U32)


def _unpack_words(w):
    return (pltpu.bitcast(w << 16, F32), pltpu.bitcast(w & jnp.uint32(0xFFFF0000), F32))


def _inproj_kernel(x_ref, g_ref, w_ref, q_ref, kr_ref, vr_ref, kb_ref, vb_ref, xr_ref, gate_ref, xn_sc):
    j = pl.program_id(1)
    tm = x_ref.shape[0]

    @pl.when(j == 0)
    def _():
        xn_sc[...] = _rms(x_ref[...], g_ref[...]).astype(BF16)

    acc = jnp.dot(xn_sc[...], w_ref[...], preferred_element_type=F32)

    @pl.when(j == 0)
    def _():
        q_ref[...] = (acc * (HEAD_DIM ** -0.5)).astype(BF16)

    for jj, rows_ref, bf_ref in ((1, kr_ref, kb_ref), (2, vr_ref, vb_ref)):
        @pl.when(j == jj)
        def _(rows_ref=rows_ref, bf_ref=bf_ref):
            rows_ref[...] = acc.reshape(tm, N_HEADS, HEAD_W)
            bf_ref[...] = acc.astype(BF16)

    for jj, ref in ((3, xr_ref), (4, gate_ref)):
        @pl.when(j == jj)
        def _(ref=ref):
            ref[...] = acc


def _inproj(x2, g, w_bf, tm):
    n = x2.shape[0]
    row = lambda i, j: (i, 0)
    row3 = lambda i, j: (i, 0, 0)
    flat_bf = jax.ShapeDtypeStruct((n, PROJ_W), BF16)
    flat_f32 = jax.ShapeDtypeStruct((n, PROJ_W), F32)
    heads = jax.ShapeDtypeStruct((n, N_HEADS, HEAD_W), F32)
    flat_spec = pl.BlockSpec((tm, PROJ_W), row)
    head_spec = pl.BlockSpec((tm, N_HEADS, HEAD_W), row3)
    return pl.pallas_call(
        _inproj_kernel,
        grid=(n // tm, N_PROJ),
        in_specs=[pl.BlockSpec((tm, D_MODEL), row),
                  pl.BlockSpec((1, D_MODEL), lambda i, j: (0, 0)),
                  pl.BlockSpec((D_MODEL, PROJ_W), lambda i, j: (0, j))],
        out_specs=[flat_spec, head_spec, head_spec, flat_spec, flat_spec, flat_spec, flat_spec],
        out_shape=[flat_bf, heads, heads, flat_bf, flat_bf, flat_f32, flat_f32],
        scratch_shapes=[pltpu.VMEM((tm, D_MODEL), BF16)],
        compiler_params=_cparams("parallel", "arbitrary"),
        name="inproj",
    )(x2, g, w_bf)


def _split_maps(q):
    lane = lax.broadcasted_iota(I32, q.shape, 1)
    zero = jnp.zeros_like(q)
    return jnp.concatenate([jnp.where(lane < HEAD_DIM, q, zero),
                            jnp.where(lane >= HEAD_DIM, q, zero)], axis=0)


def _qk(q2, kj):
    return lax.dot_general(q2, kj, (((1,), (1,)), ((), ())), preferred_element_type=F32)


def _prompt_attn_kernel(lam_ref, q_ref, k_ref, v_ref, t0_ref, t1_ref, g_ref, wa_ref, wb_ref,
                        o_ref, wa_out_ref, wb_out_ref, vt_sc, m_sc, l_sc, acc_sc, *, tile, out_scale):
    i = pl.program_id(2)
    n_kv = vt_sc.shape[0]

    wa_out_ref[...] = wa_ref[...].astype(BF16)
    wb_out_ref[...] = wb_ref[...].astype(BF16)

    @pl.when(i == 0)
    def _():
        for j in range(n_kv):
            vt_sc[j] = v_ref[j * tile:(j + 1) * tile, :].astype(F32).T.astype(BF16)

    q_t = _split_maps(q_ref[...]).astype(F32).T.astype(BF16)
    m_sc[...] = jnp.full(m_sc.shape, NEG_INF, F32)
    l_sc[...] = jnp.zeros(l_sc.shape, F32)
    acc_sc[...] = jnp.zeros(acc_sc.shape, F32)

    def block(first, count, bias_t):
        start = pl.multiple_of(first * tile, tile)
        kj = k_ref[pl.ds(start, count * tile), :]
        s = jnp.dot(kj, q_t, preferred_element_type=F32)
        if bias_t is not None:
            s = s + jnp.concatenate([bias_t, bias_t], axis=1)
        m_prev = m_sc[...]
        m_new = jnp.maximum(m_prev, jnp.max(s, axis=0, keepdims=True))
        alpha = jnp.exp(m_prev - m_new)
        p = jnp.exp(s - m_new)
        l_sc[...] = alpha * l_sc[...] + jnp.sum(p, axis=0, keepdims=True)
        pb = p.astype(BF16)
        pv = jnp.dot(vt_sc[first], pb[0:tile], preferred_element_type=F32)
        for u in range(1, count):
            pv = pv + jnp.dot(vt_sc[first + u], pb[u * tile:(u + 1) * tile], preferred_element_type=F32)
        acc_sc[...] = alpha * acc_sc[...] + pv
        m_sc[...] = m_new

    n_far = i - 1

    def pair_body(jj, c):
        block(2 * jj, 2, None)
        return c

    lax.fori_loop(0, jnp.maximum(n_far, 0) // 2, pair_body, 0)

    @pl.when(jnp.logical_and(n_far > 0, n_far % 2 == 1))
    def _():
        block(n_far - 1, 1, None)

    @pl.when(i >= 1)
    def _():
        block(i - 1, 1, t1_ref[0])

    block(i, 1, t0_ref[0])

    o = acc_sc[...] / l_sc[...]
    d = o[:, :tile] - lam_ref[0] * o[:, tile:]
    y = d * lax.rsqrt(jnp.mean(d * d, axis=0, keepdims=True) + EPS) * out_scale
    o_ref[...] = (y.T * g_ref[...]).astype(o_ref.dtype)


def _prompt_attention(q, k_bf, v_bf, t0, t1, lam, subln_g, w_a, w_b, batch, seq, out_scale):
    tile = ATTN_TILE
    nq = seq // tile
    n_steps = batch * N_HEADS * nq
    ra, rb = w_a.shape[0] // n_steps, w_b.shape[0] // n_steps
    assert ra * n_steps == w_a.shape[0] and rb * n_steps == w_b.shape[0]
    kern = functools.partial(_prompt_attn_kernel, tile=tile, out_scale=out_scale)
    qmap = lambda b, h, i: (b * nq + i, h)
    kvmap = lambda b, h, i: (b, h)
    tmap = lambda b, h, i: (h, 0, 0)
    wmap = lambda b, h, i: ((b * N_HEADS + h) * nq + i, 0)
    wa_spec = pl.BlockSpec((ra, w_a.shape[1]), wmap)
    wb_spec = pl.BlockSpec((rb, w_b.shape[1]), wmap)
    return pl.pallas_call(
        kern,
        grid=(batch, N_HEADS, nq),
        in_specs=[pl.BlockSpec(memory_space=pltpu.SMEM),
                  pl.BlockSpec((tile, HEAD_W), qmap),
                  pl.BlockSpec((seq, HEAD_W), kvmap),
                  pl.BlockSpec((seq, HEAD_W), kvmap),
                  pl.BlockSpec((1, tile, tile), tmap),
                  pl.BlockSpec((1, tile, tile), tmap),
                  pl.BlockSpec((1, HEAD_W), lambda b, h, i: (0, 0)),
                  wa_spec, wb_spec],
        out_specs=[pl.BlockSpec((tile, HEAD_W), qmap), wa_spec, wb_spec],
        out_shape=[jax.ShapeDtypeStruct((batch * seq, ATTN_WIDTH), BF16),
                   jax.ShapeDtypeStruct(w_a.shape, BF16), jax.ShapeDtypeStruct(w_b.shape, BF16)],
        scratch_shapes=[pltpu.VMEM((nq, HEAD_W, tile), BF16),
                        pltpu.VMEM((1, 2 * tile), F32), pltpu.VMEM((1, 2 * tile), F32),
                        pltpu.VMEM((HEAD_W, 2 * tile), F32)],
        compiler_params=_cparams("parallel", "parallel", "arbitrary"),
        name="prompt_attn",
    )(lam, q, k_bf, v_bf, t0, t1, subln_g, w_a, w_b)


def _softmax_update(s_list, v_list, m_sc, l_sc, acc_sc):
    m_prev = m_sc[...]
    m_new = m_prev
    for s in s_list:
        m_new = jnp.maximum(m_new, jnp.max(s, axis=-1, keepdims=True))
    alpha = jnp.exp(m_prev - m_new)
    l_new = alpha * l_sc[...]
    acc = alpha * acc_sc[...]
    for s, vj in zip(s_list, v_list):
        p = jnp.exp(s - m_new)
        l_new = l_new + jnp.sum(p, axis=-1, keepdims=True)
        acc = acc + jnp.dot(p.astype(BF16), vj, preferred_element_type=F32)
    l_sc[...] = l_new
    acc_sc[...] = acc
    m_sc[...] = m_new


def _sample_attn_kernel(pt_ref, lam_ref, q_ref, *refs, n_pp, out_scale):
    k_refs = refs[:n_pp]
    v_refs = refs[n_pp:2 * n_pp]
    tab_ref, ks_ref, vs_ref, tabs_ref, g_ref, o_ref, m_sc, l_sc, acc_sc = refs[2 * n_pp:]
    s_idx = pl.program_id(1)
    cols = k_refs[0].shape[1]

    @pl.when(s_idx == 0)
    def _():
        m_sc[...] = jnp.full(m_sc.shape, NEG_INF, F32)
        l_sc[...] = jnp.zeros(l_sc.shape, F32)
        acc_sc[...] = jnp.zeros(acc_sc.shape, F32)

    q2 = _split_maps(q_ref[0])
    s_list = [_qk(q2, k_refs[pp][0].astype(BF16)) + tab_ref[0, :, pp * cols:(pp + 1) * cols]
              for pp in range(n_pp)]
    _softmax_update(s_list, [v_refs[pp][0].astype(BF16) for pp in range(n_pp)], m_sc, l_sc, acc_sc)

    @pl.when(s_idx == pl.num_programs(1) - 1)
    def _():
        s = _qk(q2, ks_ref[0].astype(BF16)) + tabs_ref[...]
        _softmax_update([s], [vs_ref[0].astype(BF16)], m_sc, l_sc, acc_sc)
        o = acc_sc[...] / l_sc[...]
        r = o.shape[0] // 2
        d = o[:r] - lam_ref[0] * o[r:]
        o_ref[0] = (_rms(d, g_ref[...]) * out_scale).astype(o_ref.dtype)


def _sample_attention(q3, cache_k3, cache_v3, page_table, tab, k_self, v_self, tab_self, lam, subln_g,
                      out_scale):
    nb, rows, _ = q3.shape
    n_pages = page_table.shape[1]
    n_pp = PAGES_PER_STEP
    n_steps = n_pages // n_pp
    cols = cache_k3.shape[1]
    kern = functools.partial(_sample_attn_kernel, n_pp=n_pp, out_scale=out_scale)

    def page_map(pp):
        return lambda b, s, pt: (pt[b, s * n_pp + pp], 0, 0)

    page_specs = [pl.BlockSpec((1, cols, HEAD_W), page_map(pp)) for pp in range(n_pp)]
    bmap = lambda b, s, pt: (b, 0, 0)
    grid_spec = pltpu.PrefetchScalarGridSpec(
        num_scalar_prefetch=1,
        grid=(nb, n_steps),
        in_specs=[pl.BlockSpec(memory_space=pltpu.SMEM),
                  pl.BlockSpec((1, rows, HEAD_W), bmap)]
                 + page_specs + page_specs
                 + [pl.BlockSpec((1, 2 * rows, n_pp * cols),
                                 lambda b, s, pt: (jnp.where(s == n_steps - 1, 1, 0), 0, 0)),
                    pl.BlockSpec((1, HEAD_W, HEAD_W), bmap),
                    pl.BlockSpec((1, HEAD_W, HEAD_W), bmap),
                    pl.BlockSpec((2 * rows, HEAD_W), lambda b, s, pt: (0, 0)),
                    pl.BlockSpec((1, HEAD_W), lambda b, s, pt: (0, 0))],
        out_specs=pl.BlockSpec((1, rows, HEAD_W), bmap),
        scratch_shapes=[pltpu.VMEM((2 * rows, 1), F32), pltpu.VMEM((2 * rows, 1), F32),
                        pltpu.VMEM((2 * rows, HEAD_W), F32)],
    )
    return pl.pallas_call(
        kern,
        grid_spec=grid_spec,
        out_shape=jax.ShapeDtypeStruct((nb, rows, HEAD_W), BF16),
        compiler_params=_cparams("parallel", "arbitrary"),
        name="sample_attn",
    )(page_table, lam, q3, *([cache_k3] * n_pp), *([cache_v3] * n_pp), tab, k_self, v_self, tab_self,
      subln_g)


def _softplus(x):
    return jnp.maximum(x, 0.0) + jnp.log1p(jnp.exp(-jnp.abs(x)))


def _gelu_tanh(x):
    return 0.5 * x * (1.0 + jnp.tanh(math.sqrt(2.0 / math.pi) * (x + 0.044715 * x * x * x)))


def _block_gate(cb, w_ref, b):
    parts = [jnp.dot(cb[:, n * LRU_BLOCK_W:(n + 1) * LRU_BLOCK_W], w_ref[n], preferred_element_type=F32)
             for n in range(LRU_BLOCKS)]
    return jax.nn.sigmoid(jnp.concatenate(parts, axis=-1) + b)


def _lru_coeffs(conv, wrg_ref, brg, wig_ref, big, lam):
    cb = conv.astype(BF16)
    rg = _block_gate(cb, wrg_ref, brg)
    ig = _block_gate(cb, wig_ref, big)
    log_a = -LRU_C * rg * _softplus(-lam)
    a = jnp.exp(log_a)
    bx = jnp.sqrt(jnp.tanh(-log_a) * (1.0 + a * a)) * ig * conv
    return a, bx


def _prompt_lru_kernel(xr_ref, gate_ref, cw_ref, cb_ref, wrg_ref, brg_ref, wig_ref, big_ref, lam_ref,
                       y_ref, conv_ref, h_ref, xc_sc, h_sc, *, tile):
    t = pl.program_id(1)
    pad = SUBLANES

    @pl.when(t == 0)
    def _():
        xc_sc[0:pad, :] = jnp.zeros((pad, LRU_WIDTH), F32)
        h_sc[...] = jnp.zeros(h_sc.shape, F32)

    xr = xr_ref[...]
    xc_sc[pad:pad + tile, :] = xr
    conv = cb_ref[...]
    for j in range(CONV_W):
        off = pad - (CONV_W - 1) + j
        conv = conv + xc_sc[off:off + tile, :] * cw_ref[j:j + 1, :]
    xc_sc[0:pad, :] = xr[tile - pad:, :]

    a, bx = _lru_coeffs(conv, wrg_ref, brg_ref[...], wig_ref, big_ref[...], lam_ref[...])

    row = lax.broadcasted_iota(I32, a.shape, 0)
    d = 1
    while d < tile:
        keep = row >= d
        a_sh = pltpu.roll(a, d, 0)
        b_sh = pltpu.roll(bx, d, 0)
        bx = jnp.where(keep, a * b_sh + bx, bx)
        a = jnp.where(keep, a * a_sh, a)
        d *= 2
    h = bx + a * h_sc[...]
    h_sc[...] = h[tile - 1:tile, :]
    y_ref[...] = (h * _gelu_tanh(gate_ref[...])).astype(y_ref.dtype)

    @pl.when(t == pl.num_programs(1) - 1)
    def _():
        conv_ref[0] = xc_sc[pad + tile - (CONV_W - 1):pad + tile, :]
        h_ref[0] = h[tile - 1:tile, :]


def _prompt_lru(xr, gate, conv_w, conv_b, w_rg, b_rg, w_ig, b_ig, lru_lambda, batch, seq):
    tile = LRU_TILE
    nt = seq // tile
    kern = functools.partial(_prompt_lru_kernel, tile=tile)
    rmap = lambda b, t: (b * nt + t, 0)
    full2 = lambda b, t: (0, 0)
    full3 = lambda b, t: (0, 0, 0)
    bmap = lambda b, t: (b, 0, 0)
    vec = pl.BlockSpec((1, LRU_WIDTH), full2)
    wspec = pl.BlockSpec((LRU_BLOCKS, LRU_BLOCK_W, LRU_BLOCK_W), full3)
    return pl.pallas_call(
        kern,
        grid=(batch, nt),
        in_specs=[pl.BlockSpec((tile, LRU_WIDTH), rmap), pl.BlockSpec((tile, LRU_WIDTH), rmap),
                  pl.BlockSpec((CONV_W, LRU_WIDTH), full2), vec, wspec, vec, wspec, vec, vec],
        out_specs=[pl.BlockSpec((tile, LRU_WIDTH), rmap),
                   pl.BlockSpec((1, CONV_W - 1, LRU_WIDTH), bmap),
                   pl.BlockSpec((1, 1, LRU_WIDTH), bmap)],
        out_shape=[jax.ShapeDtypeStruct((batch * seq, LRU_WIDTH), BF16),
                   jax.ShapeDtypeStruct((batch, CONV_W - 1, LRU_WIDTH), F32),
                   jax.ShapeDtypeStruct((batch, 1, LRU_WIDTH), F32)],
        scratch_shapes=[pltpu.VMEM((tile + SUBLANES, LRU_WIDTH), F32), pltpu.VMEM((1, LRU_WIDTH), F32)],
        compiler_params=_cparams("parallel", "arbitrary"),
        name="prompt_lru",
    )(xr, gate, conv_w, conv_b, w_rg, b_rg, w_ig, b_ig, lru_lambda)


def _sample_lru_kernel(xr_ref, gate_ref, cprev_ref, hprev_ref, cw_ref, cb_ref, wrg_ref, brg_ref,
                       wig_ref, big_ref, lam_ref, y_ref, conv_ref, h_ref, *, steps):
    xcat = [cprev_ref[j] for j in range(CONV_W - 1)] + [xr_ref[t] for t in range(steps)]
    h = hprev_ref[...]
    for t in range(steps):
        conv = cb_ref[...]
        for j in range(CONV_W):
            conv = conv + xcat[t + j] * cw_ref[j:j + 1, :]
        a, bx = _lru_coeffs(conv, wrg_ref, brg_ref[...], wig_ref, big_ref[...], lam_ref[...])
        h = a * h + bx
        y_ref[t] = (h * _gelu_tanh(gate_ref[t])).astype(y_ref.dtype)
    for j in range(CONV_W - 1):
        conv_ref[j] = xcat[steps + j]
    h_ref[...] = h


def _sample_lru(xr_t, gate_t, cprev_t, hprev, conv_w, conv_b, w_rg, b_rg, w_ig, b_ig, lru_lambda):
    steps, nb, _ = xr_t.shape
    kern = functools.partial(_sample_lru_kernel, steps=steps)
    return pl.pallas_call(
        kern,
        out_shape=[jax.ShapeDtypeStruct((steps, nb, LRU_WIDTH), BF16),
                   jax.ShapeDtypeStruct((CONV_W - 1, nb, LRU_WIDTH), F32),
                   jax.ShapeDtypeStruct((nb, LRU_WIDTH), F32)],
        compiler_params=pltpu.CompilerParams(vmem_limit_bytes=VMEM_LIMIT),
        name="sample_lru",
    )(xr_t, gate_t, cprev_t, hprev, conv_w, conv_b, w_rg, b_rg, w_ig, b_ig, lru_lambda)


def _split_bf16(x):
    hi = x.astype(BF16)
    return hi, (x - hi.astype(F32)).astype(BF16)


def _outproj_kernel(a_ref, l_ref, w_ref, x_ref, g_ref, rw_ref, rb_ref, xmid_ref, xp_ref, logit_ref):
    tm = x_ref.shape[0]
    y = jnp.dot(a_ref[...], w_ref[0:ATTN_WIDTH, :], preferred_element_type=F32)
    y = y + jnp.dot(l_ref[...], w_ref[ATTN_WIDTH:, :], preferred_element_type=F32)
    xm = x_ref[...] + y
    xmid_ref[...] = xm
    xn = _rms(xm, g_ref[...])
    xn_hi, xn_lo = _split_bf16(xn)
    bits = _bf16_bits(xn)
    words = bits[:, PACK_W:] | (bits[:, :PACK_W] >> 16)
    xp_ref[...] = words.reshape(tm, PACK_S, LANES)
    rw_hi, rw_lo = _split_bf16(rw_ref[...])
    logits = (jnp.dot(xn_hi, rw_hi, preferred_element_type=F32)
              + jnp.dot(xn_lo, rw_hi, preferred_element_type=F32)
              + jnp.dot(xn_hi, rw_lo, preferred_element_type=F32))
    logit_ref[...] = logits + rb_ref[...]


def _outproj(attn, lru, w_out_bf, x2, g, router_w, router_b, tm):
    n = x2.shape[0]
    row = lambda i: (i, 0)
    full = lambda i: (0, 0)
    return pl.pallas_call(
        _outproj_kernel,
        grid=(n // tm,),
        in_specs=[pl.BlockSpec((tm, ATTN_WIDTH), row), pl.BlockSpec((tm, LRU_WIDTH), row),
                  pl.BlockSpec((D_MODEL, D_MODEL), full, pipeline_mode=pl.Buffered(1)),
                  pl.BlockSpec((tm, D_MODEL), row),
                  pl.BlockSpec((1, D_MODEL), full), pl.BlockSpec((D_MODEL, N_EXPERTS), full),
                  pl.BlockSpec((1, N_EXPERTS), full)],
        out_specs=[pl.BlockSpec((tm, D_MODEL), row), pl.BlockSpec((tm, PACK_S, LANES), lambda i: (i, 0, 0)),
                   pl.BlockSpec((tm, N_EXPERTS), row)],
        out_shape=[jax.ShapeDtypeStruct((n, D_MODEL), F32), jax.ShapeDtypeStruct((n, PACK_S, LANES), U32),
                   jax.ShapeDtypeStruct((n, N_EXPERTS), F32)],
        compiler_params=_cparams("parallel"),
        name="outproj",
    )(attn, lru, w_out_bf, x2, g, router_w, router_b)


def _route_kernel(logit_ref, base_ref, idx_ref, gate_ref, rank_ref, cnt_ref, cnt_sc):
    i = pl.program_id(0)
    tr = logit_ref.shape[0]

    @pl.when(i == 0)
    def _():
        cnt_sc[...] = base_ref[...]

    logits = logit_ref[...]
    lane = lax.broadcasted_iota(I32, logits.shape, 1)
    vals, firsts, hots = [], [], []
    for _ in range(TOP_K):
        mx = jnp.max(logits, axis=-1, keepdims=True)
        first = jnp.min(jnp.where(logits == mx, lane, N_EXPERTS), axis=-1, keepdims=True)
        hot = lane == first
        vals.append(mx)
        firsts.append(first)
        hots.append(hot)
        logits = jnp.where(hot, -jnp.inf, logits)

    member = jnp.zeros(logits.shape, F32)
    for hot in hots:
        member = member + hot.astype(F32)
    r_i = lax.broadcasted_iota(I32, (tr, tr), 0)
    c_i = lax.broadcasted_iota(I32, (tr, tr), 1)
    earlier = jnp.where(c_i < r_i, 1.0, 0.0).astype(BF16)
    before = cnt_sc[...] + jnp.dot(earlier, member.astype(BF16), preferred_element_type=F32)

    exps = [jnp.exp(v - vals[0]) for v in vals]
    denom = exps[0]
    for e in exps[1:]:
        denom = denom + e
    for k in range(TOP_K):
        idx_ref[:, k:k + 1] = firsts[k]
        gate_ref[:, k:k + 1] = exps[k] / denom
        rank = jnp.sum(jnp.where(hots[k], before, 0.0), axis=-1, keepdims=True)
        rank_ref[:, k:k + 1] = rank.astype(I32)
    cnt_sc[...] = cnt_sc[...] + jnp.sum(member, axis=0, keepdims=True)

    @pl.when(i == pl.num_programs(0) - 1)
    def _():
        cnt_ref[...] = cnt_sc[...]


def _route(logits, base_counts, tr):
    n = logits.shape[0]
    row = lambda i: (i, 0)
    full = lambda i: (0, 0)
    k_spec = pl.BlockSpec((tr, TOP_K), row)
    return pl.pallas_call(
        _route_kernel,
        grid=(n // tr,),
        in_specs=[pl.BlockSpec((tr, N_EXPERTS), row), pl.BlockSpec((1, N_EXPERTS), full)],
        out_specs=[k_spec, k_spec, k_spec, pl.BlockSpec((1, N_EXPERTS), full)],
        out_shape=[jax.ShapeDtypeStruct((n, TOP_K), I32), jax.ShapeDtypeStruct((n, TOP_K), F32),
                   jax.ShapeDtypeStruct((n, TOP_K), I32), jax.ShapeDtypeStruct((1, N_EXPERTS), F32)],
        scratch_shapes=[pltpu.VMEM((1, N_EXPERTS), F32)],
        compiler_params=_cparams("arbitrary"),
        name="route",
    )(logits, base_counts)


def _dispatch_kernel(pos_ref, x_ref, xs_in_ref, xs_ref, sem):
    del xs_in_ref
    tm = x_ref.shape[0]

    def issue(r, c):
        for k in range(TOP_K):
            pltpu.make_async_copy(x_ref.at[r], xs_ref.at[pos_ref[r * TOP_K + k]], sem).start()
        return c

    lax.fori_loop(0, tm, issue, 0)
    for _ in range(TOP_K):
        pltpu.make_async_copy(x_ref, xs_ref.at[pl.ds(0, tm)], sem).wait()


def _dispatch(pos_flat, x_packed, x_sorted, tm):
    n = x_packed.shape[0]
    return pl.pallas_call(
        _dispatch_kernel,
        grid=(n // tm,),
        in_specs=[pl.BlockSpec((tm * TOP_K,), lambda i: (i,), memory_space=pltpu.SMEM),
                  pl.BlockSpec((tm, PACK_S, LANES), lambda i: (i, 0, 0)),
                  pl.BlockSpec(memory_space=pl.ANY)],
        out_specs=pl.BlockSpec(memory_space=pl.ANY),
        out_shape=jax.ShapeDtypeStruct(x_sorted.shape, x_sorted.dtype),
        scratch_shapes=[pltpu.SemaphoreType.DMA],
        input_output_aliases={2: 0},
        compiler_params=_cparams("arbitrary"),
        name="moe_dispatch",
    )(pos_flat, x_packed, x_sorted)


def _moe_kernel(te_ref, na_ref, x_ref, wg_ref, wu_ref, wd_ref, bg_ref, bu_ref, bd_ref, o_ref,
                xb_sc, act_sc, lo_sc, *, n_f):
    i = pl.program_id(0)
    j = pl.program_id(1)
    tm = x_ref.shape[0]
    tf = act_sc.shape[2]
    last = pl.num_programs(1) - 1
    active = i < na_ref[0]

    @pl.when(jnp.logical_and(active, j == 0))
    def _():
        lo, hi = _unpack_words(x_ref[...].reshape(tm, PACK_W))
        xb_sc[:, :PACK_W] = lo.astype(BF16)
        xb_sc[:, PACK_W:] = hi.astype(BF16)

    @pl.when(jnp.logical_and(active, j < n_f))
    def _():
        x = xb_sc[...]
        glu = jnp.dot(x, wg_ref[0], preferred_element_type=F32) + bg_ref[0]
        lin = jnp.dot(x, wu_ref[0], preferred_element_type=F32) + bu_ref[0]
        glu = jnp.minimum(glu, SWIGLU_LIMIT)
        lin = jnp.clip(lin, -SWIGLU_LIMIT, SWIGLU_LIMIT)
        act_sc[j] = (glu * jax.nn.sigmoid(SWIGLU_ALPHA * glu) * (lin + 1.0)).astype(BF16)

    @pl.when(jnp.logical_and(active, j >= n_f))
    def _():
        y = bd_ref[0] + jnp.dot(act_sc[0], wd_ref[0, 0:tf, :], preferred_element_type=F32)
        for f in range(1, n_f):
            y = y + jnp.dot(act_sc[f], wd_ref[0, f * tf:(f + 1) * tf, :], preferred_element_type=F32)
        bits = _bf16_bits(y)

        @pl.when(j == n_f)
        def _():
            lo_sc[...] = bits >> 16

        @pl.when(j == last)
        def _():
            o_ref[...] = (bits | lo_sc[...]).reshape(tm, PACK_S, LANES)

    @pl.when(jnp.logical_and(jnp.logical_not(active), j == last))
    def _():
        o_ref[...] = jnp.zeros(o_ref.shape, U32)


def _moe_ffn(x_sorted, tile_expert, n_active, w_gate_up, b_gate_up, w_down, b_down):
    n_slots = x_sorted.shape[0]
    tm, tf = MOE_TILE, MOE_FF_TILE
    n_tiles = n_slots // tm
    n_f = D_FF // tf
    n_d = D_MODEL // PACK_W
    kern = functools.partial(_moe_kernel, n_f=n_f)

    def fidx(i, j, na):
        return jnp.where(i < na[0], jnp.minimum(j, n_f - 1), n_f - 1)

    def down_block(i, j, te, na):
        in_down = jnp.logical_and(i < na[0], j >= n_f)
        e = jnp.where(in_down, te[i], te[jnp.maximum(i - 1, 0)])
        return e, 0, jnp.where(in_down, j - n_f, n_d - 1)

    def rowidx(i, na):
        return jnp.minimum(i, na[0] - 1)

    grid_spec = pltpu.PrefetchScalarGridSpec(
        num_scalar_prefetch=2,
        grid=(n_tiles, n_f + n_d),
        in_specs=[pl.BlockSpec((tm, PACK_S, LANES), lambda i, j, te, na: (rowidx(i, na), 0, 0)),
                  pl.BlockSpec((1, D_MODEL, tf), lambda i, j, te, na: (te[i], 0, fidx(i, j, na))),
                  pl.BlockSpec((1, D_MODEL, tf), lambda i, j, te, na: (te[i], 0, n_f + fidx(i, j, na))),
                  pl.BlockSpec((1, D_FF, PACK_W), down_block),
                  pl.BlockSpec((1, 1, tf), lambda i, j, te, na: (te[i], 0, fidx(i, j, na))),
                  pl.BlockSpec((1, 1, tf), lambda i, j, te, na: (te[i], 0, n_f + fidx(i, j, na))),
                  pl.BlockSpec((1, 1, PACK_W), down_block)],
        out_specs=pl.BlockSpec((tm, PACK_S, LANES), lambda i, j, te, na: (i, 0, 0)),
        scratch_shapes=[pltpu.VMEM((tm, D_MODEL), BF16), pltpu.VMEM((n_f, tm, tf), BF16),
                        pltpu.VMEM((tm, PACK_W), U32)],
    )
    bgu3 = b_gate_up.reshape(N_EXPERTS, 1, 2 * D_FF)
    bd3 = b_down.reshape(N_EXPERTS, 1, D_MODEL)
    return pl.pallas_call(
        kern,
        grid_spec=grid_spec,
        out_shape=jax.ShapeDtypeStruct((n_slots, PACK_S, LANES), U32),
        compiler_params=_cparams("arbitrary", "arbitrary"),
        name="moe_ffn",
    )(tile_expert, n_active, x_sorted, w_gate_up, w_gate_up, w_down, bgu3, bgu3, bd3)


def _combine_kernel(pos_ref, pos_next_ref, gate_ref, x_ref, g_ref, y_hbm, o_ref, ybuf, sems):
    i = pl.program_id(0)
    n = pl.num_programs(0)
    tm = x_ref.shape[0]
    slot = i % 2
    rows_per_group = min(COMBINE_GROUP, tm)

    def issue_group(p_ref, buf_slot, r0):
        for rr in range(rows_per_group):
            r = r0 + rr
            for k in range(TOP_K):
                pltpu.make_async_copy(y_hbm.at[p_ref[r * TOP_K + k]], ybuf.at[buf_slot, k, r],
                                      sems.at[buf_slot]).start()

    def wait_slot(buf_slot):
        for k in range(TOP_K):
            pltpu.make_async_copy(y_hbm.at[pl.ds(0, tm)], ybuf.at[buf_slot, k], sems.at[buf_slot]).wait()

    @pl.when(i == 0)
    def _():
        def first(gi, c):
            issue_group(pos_ref, 0, gi * rows_per_group)
            return c
        lax.fori_loop(0, tm // rows_per_group, first, 0)

    wait_slot(slot)

    def group(gi, c):
        r0 = pl.multiple_of(gi * rows_per_group, rows_per_group)
        issue_group(pos_next_ref, 1 - slot, r0)
        rows = pl.ds(r0, rows_per_group)
        gates = gate_ref[rows, :]
        x = x_ref[rows, :]
        out_lo, out_hi = x[:, :PACK_W], x[:, PACK_W:]
        for k in range(TOP_K):
            lo, hi = _unpack_words(ybuf[slot, k, rows].reshape(rows_per_group, PACK_W))
            out_lo = out_lo + gates[:, k:k + 1] * lo
            out_hi = out_hi + gates[:, k:k + 1] * hi
        o_ref[rows, :] = _rms(jnp.concatenate([out_lo, out_hi], axis=1), g_ref[...])
        return c

    lax.fori_loop(0, tm // rows_per_group, group, 0)

    @pl.when(i == n - 1)
    def _():
        wait_slot(1 - slot)


def _combine(pos_flat, gates, x_mid, g, y_sorted, tm):
    n = x_mid.shape[0]
    steps = n // tm
    row = lambda i: (i, 0)
    return pl.pallas_call(
        _combine_kernel,
        grid=(steps,),
        in_specs=[pl.BlockSpec((tm * TOP_K,), lambda i: (i,), memory_space=pltpu.SMEM),
                  pl.BlockSpec((tm * TOP_K,), lambda i: (jnp.minimum(i + 1, steps - 1),),
                               memory_space=pltpu.SMEM),
                  pl.BlockSpec((tm, TOP_K), row),
                  pl.BlockSpec((tm, D_MODEL), row),
                  pl.BlockSpec((1, D_MODEL), lambda i: (0, 0)),
                  pl.BlockSpec(memory_space=pl.ANY)],
        out_specs=pl.BlockSpec((tm, D_MODEL), row),
        out_shape=jax.ShapeDtypeStruct((n, D_MODEL), F32),
        scratch_shapes=[pltpu.VMEM((2, TOP_K, tm, PACK_S, LANES), U32), pltpu.SemaphoreType.DMA((2,))],
        compiler_params=_cparams("arbitrary"),
        name="moe_combine",
    )(pos_flat, pos_flat, gates, x_mid, g, y_sorted)


def _bias_of_distance(rel_table, n):
    nf = jnp.maximum(n, 1).astype(F32)
    large = REL_EXACT + (jnp.log(nf / REL_EXACT) / math.log(REL_MAX_DIST / REL_EXACT)
                         * (REL_BUCKETS - REL_EXACT)).astype(I32)
    bucket = jnp.where(n < REL_EXACT, n, jnp.minimum(large, REL_BUCKETS - 1))
    hot = bucket[..., None, None] == jnp.arange(REL_BUCKETS, dtype=I32)[:, None]
    return jnp.sum(jnp.where(hot, rel_table.astype(F32), 0.0), axis=-2)


def _prompt_bias_tables(rel_table, tile):
    far = rel_table[REL_BUCKETS - 1].astype(F32)
    d = jnp.arange(tile)[None, :] - jnp.arange(tile)[:, None]
    diag = jnp.where((d >= 0)[..., None], _bias_of_distance(rel_table, jnp.maximum(d, 0)) - far, NEG_INF)
    near = _bias_of_distance(rel_table, d + tile) - far
    return jnp.transpose(diag, (2, 0, 1)), jnp.transpose(near, (2, 0, 1))


def _sample_bias_tables(rel_table, dec_seq, past_len, page_size):
    heads = jnp.arange(N_HEADS)
    same_head = heads[:, None] == heads[None, :]

    def table(dist, valid):
        b = _bias_of_distance(rel_table, jnp.maximum(dist, 0))
        b = jnp.transpose(b, (0, 2, 1))
        full = jnp.where(same_head[None, :, None, :] & valid[:, None, :, None],
                         b[:, :, :, None], NEG_INF)
        full = full.reshape(dec_seq * N_HEADS, -1)
        return jnp.concatenate([full, full], axis=0)

    qpos = past_len + jnp.arange(dec_seq)
    span = PAGES_PER_STEP * page_size
    always = jnp.ones((dec_seq, span), bool)
    k_last = past_len - span + jnp.arange(span)
    tab_far = table(jnp.full((dec_seq, span), REL_MAX_DIST), always)
    tab_last = table(qpos[:, None] - k_last[None, :], always)
    n_self = HEAD_W // N_HEADS
    tk = jnp.arange(n_self)
    dist = jnp.arange(dec_seq)[:, None] - tk[None, :]
    tab_self = table(dist, (dist >= 0) & (tk[None, :] < dec_seq))
    return jnp.stack([tab_far, tab_last]), tab_self


def _tile_plan(counts, n_assign, tm):
    counts = counts.reshape(N_EXPERTS).astype(I32)
    padded = (counts + tm - 1) // tm * tm
    pad_end = jnp.cumsum(padded)
    pad_start = pad_end - padded
    n_tiles = n_assign // tm + N_EXPERTS
    n_active = (pad_end[-1] // tm).astype(I32)
    tile_start = jnp.minimum(jnp.arange(n_tiles, dtype=I32), n_active - 1) * tm
    tile_expert = jnp.minimum(jnp.sum((pad_end[None, :] <= tile_start[:, None]).astype(I32), axis=1),
                              N_EXPERTS - 1)
    return pad_start, tile_expert, n_active.reshape(1), n_tiles


def _slots(pad_start, idx, rank):
    hot = idx[..., None] == jnp.arange(N_EXPERTS, dtype=I32)
    return (jnp.sum(jnp.where(hot, pad_start, 0), axis=-1) + rank).reshape(-1)


def kernel(x_prompt, x_sample, cache_k, cache_v, state_conv, state_h, page_table, rel_bias, norm_mix_g, w_in, lam_q1, lam_k1, lam_q2, lam_k2, subln_g, conv_w, conv_b, w_rg, b_rg, w_ig, b_ig, lru_lambda, w_out, norm_ffn_g, router_w, router_b, w_gate_up, b_gate_up, w_down, b_down, norm_final_g):
    depth = w_in.shape[0]
    assert depth == 1
    layer = 0
    bp, seq, _ = x_prompt.shape
    bs, dec_seq, _ = x_sample.shape
    n_pool, page_size = cache_k.shape[1], cache_k.shape[2]
    past_len = page_table.shape[1] * page_size
    n_p, n_s = bp * seq, bs * dec_seq
    assert ATTN_TILE >= REL_MAX_DIST and PAGES_PER_STEP * page_size >= REL_MAX_DIST

    lam_init = 0.8 - 0.6 * math.exp(-0.3 * layer)
    out_scale = 1.0 - lam_init
    lam = (jnp.exp(jnp.sum(lam_q1[layer] * lam_k1[layer])) - jnp.exp(jnp.sum(lam_q2[layer] * lam_k2[layer]))
           + lam_init).reshape(1).astype(F32)

    g_mix = norm_mix_g[layer].reshape(1, D_MODEL)
    g_ffn = norm_ffn_g[layer].reshape(1, D_MODEL)
    g_fin = norm_final_g.reshape(1, D_MODEL)
    g_sub = subln_g[layer].reshape(1, HEAD_W)
    w_in_bf = w_in[layer].astype(BF16)
    w_out_bf = w_out[layer].astype(BF16)
    lru_args = (conv_w[layer], conv_b[layer].reshape(1, LRU_WIDTH),
                w_rg[layer].astype(BF16), b_rg[layer].reshape(1, LRU_WIDTH),
                w_ig[layer].astype(BF16), b_ig[layer].reshape(1, LRU_WIDTH),
                lru_lambda[layer].reshape(1, LRU_WIDTH))
    rw = router_w[layer]
    rb = router_b[layer].reshape(1, N_EXPERTS)

    xp2 = x_prompt.reshape(n_p, D_MODEL)
    xs2 = x_sample.reshape(n_s, D_MODEL)

    q_p, k_p, v_p, kb_p, vb_p, xr_p, gate_p = _inproj(xp2, g_mix, w_in_bf, ROW_TILE)
    t0, t1 = _prompt_bias_tables(rel_bias, ATTN_TILE)
    attn_p, wgu_bf, wd_bf = _prompt_attention(
        q_p, kb_p, vb_p, t0, t1, lam, g_sub,
        w_gate_up[layer].reshape(N_EXPERTS * D_MODEL, 2 * D_FF), w_down[layer].reshape(N_EXPERTS * D_FF, D_MODEL),
        bp, seq, out_scale)
    lru_p, conv_p, h_p = _prompt_lru(xr_p, gate_p, *lru_args, bp, seq)
    xmid_p, xpk_p, logit_p = _outproj(attn_p, lru_p, w_out_bf, xp2, g_ffn, rw, rb, ROW_TILE)

    q_s, k_s, v_s, _, _, xr_s, gate_s = _inproj(xs2, g_mix, w_in_bf, n_s)
    tab, tab_self = _sample_bias_tables(rel_bias, dec_seq, past_len, page_size)
    rows = dec_seq * N_HEADS
    pad_rows = HEAD_W - rows

    def self_rows(a):
        a3 = a.reshape(bs, rows, HEAD_W)
        return jnp.concatenate([a3, jnp.zeros((bs, pad_rows, HEAD_W), a.dtype)], axis=1)

    attn_s = _sample_attention(
        q_s.reshape(bs, rows, HEAD_W),
        cache_k[layer].reshape(n_pool, page_size * N_HEADS, HEAD_W),
        cache_v[layer].reshape(n_pool, page_size * N_HEADS, HEAD_W),
        page_table, tab, self_rows(k_s), self_rows(v_s), tab_self, lam, g_sub, out_scale)
    attn_s = attn_s.reshape(n_s, ATTN_WIDTH)

    def tmajor(a):
        return jnp.transpose(a.reshape(bs, dec_seq, LRU_WIDTH), (1, 0, 2))

    lru_s_t, conv_s_t, h_s = _sample_lru(tmajor(xr_s), tmajor(gate_s),
                                         jnp.transpose(state_conv[layer], (1, 0, 2)), state_h[layer],
                                         *lru_args)
    lru_s = jnp.transpose(lru_s_t, (1, 0, 2)).reshape(n_s, LRU_WIDTH)
    conv_s = jnp.transpose(conv_s_t, (1, 0, 2))
    xmid_s, xpk_s, logit_s = _outproj(attn_s, lru_s, w_out_bf, xs2, g_ffn, rw, rb, n_s)

    idx_p, gates_p, rank_p, cnt_p = _route(logit_p, jnp.zeros((1, N_EXPERTS), F32), ROW_TILE)
    idx_s, gates_s, rank_s, cnt = _route(logit_s, cnt_p, n_s)
    n_assign = (n_p + n_s) * TOP_K
    pad_start, tile_expert, n_active, n_tiles = _tile_plan(cnt, n_assign, MOE_TILE)
    pos_p = _slots(pad_start, idx_p, rank_p)
    pos_s = _slots(pad_start, idx_s, rank_s)
    x_sorted = jnp.zeros((n_tiles * MOE_TILE, PACK_S, LANES), U32)
    x_sorted = _dispatch(pos_p, xpk_p, x_sorted, ROW_TILE)
    x_sorted = _dispatch(pos_s, xpk_s, x_sorted, n_s)
    y_sorted = _moe_ffn(x_sorted, tile_expert, n_active, wgu_bf.reshape(N_EXPERTS, D_MODEL, 2 * D_FF),
                        b_gate_up[layer], wd_bf.reshape(N_EXPERTS, D_FF, D_MODEL), b_down[layer])
    y_p = _combine(pos_p, gates_p, xmid_p, g_fin, y_sorted, COMBINE_TILE)
    y_s = _combine(pos_s, gates_s, xmid_s, g_fin, y_sorted, n_s)

    kv_p = (depth, bp, seq, N_HEADS, HEAD_W)
    kv_s = (depth, bs, dec_seq, N_HEADS, HEAD_W)
    return (y_p.reshape(bp, seq, D_MODEL), y_s.reshape(bs, dec_seq, D_MODEL),
            k_p.reshape(kv_p), v_p.reshape(kv_p),
            conv_p.reshape(depth, bp, CONV_W - 1, LRU_WIDTH), h_p.reshape(depth, bp, LRU_WIDTH),
            k_s.reshape(kv_s), v_s.reshape(kv_s),
            conv_s.reshape(depth, bs, CONV_W - 1, LRU_WIDTH), h_s.reshape(depth, bs, LRU_WIDTH))
```

```python
import functools
import math

import jax
import jax.numpy as jnp
from jax import lax
from jax.experimental import pallas as pl
from jax.experimental.pallas import tpu as pltpu

F32 = jnp.float32
BF16 = jnp.bfloat16
U32 = jnp.uint32
I32 = jnp.int32

D_MODEL = 2048
ATTN_WIDTH = D_MODEL // 2
LRU_WIDTH = D_MODEL - ATTN_WIDTH
HEAD_DIM = 64
HEAD_W = 2 * HEAD_DIM
N_HEADS = ATTN_WIDTH // HEAD_W
LRU_BLOCKS = 8
LRU_BLOCK_W = LRU_WIDTH // LRU_BLOCKS
CONV_W = 4
LRU_C = 8.0
REL_BUCKETS = 32
REL_EXACT = REL_BUCKETS // 2
REL_MAX_DIST = 128
N_EXPERTS = 32
TOP_K = 4
D_FF = D_MODEL
SWIGLU_LIMIT = 7.0
SWIGLU_ALPHA = 1.702
EPS = 1e-5
NEG_INF = -1e30
LOG2E = math.log2(math.e)
ONES_ROWS = 16
N_PROJ = 5
PROJ_W = 1024

LANES = 128
SUBLANES = 8
VMEM_LIMIT = 56 * 1024 * 1024
PACK_W = D_MODEL // 2
PACK_S = PACK_W // LANES
ROW_S = D_MODEL // LANES

ATTN_TILE = 512
LRU_TILE = 256
ROW_TILE = 512
MOE_TILE = 512
MOE_FF_TILE = 512
COMBINE_TILE = 512
COMBINE_GROUP = 32
PAGES_PER_STEP = 8


def _cparams(*sem):
    return pltpu.CompilerParams(dimension_semantics=sem, vmem_limit_bytes=VMEM_LIMIT)


def _rms(x, g):
    return x * lax.rsqrt(jnp.mean(x * x, axis=-1, keepdims=True) + EPS) * g


def _bf16_bits(x):
    return pltpu.bitcast(x.astype(BF16).astype(F32), U32)


def _unpack_words(w):
    return (pltpu.bitcast(w << 16, F32), pltpu.bitcast(w & jnp.uint32(0xFFFF0000), F32))


def _inproj_kernel(x_ref, g_ref, w_ref, q_ref, kr_ref, vr_ref, kb_ref, vb_ref, xr_ref, gate_ref, xn_sc):
    j = pl.program_id(1)
    tm = x_ref.shape[0]

    @pl.when(j == 0)
    def _():
        xn_sc[...] = _rms(x_ref[...], g_ref[...]).astype(BF16)

    def proj():
        return jnp.dot(xn_sc[...], w_ref[...], preferred_element_type=F32)

    @pl.when(j == 0)
    def _():
        q_ref[...] = (proj() * (HEAD_DIM ** -0.5)).astype(BF16)

    for jj, rows_ref, bf_ref, bf_scale in ((1, kr_ref, kb_ref, LOG2E), (2, vr_ref, vb_ref, None)):
        @pl.when(j == jj)
        def _(rows_ref=rows_ref, bf_ref=bf_ref, bf_scale=bf_scale):
            acc = proj()
            rows_ref[...] = acc.reshape(tm, N_HEADS, HEAD_W)
            bf_ref[...] = (acc if bf_scale is None else acc * bf_scale).astype(BF16)

    for jj, ref in ((3, xr_ref), (4, gate_ref)):
        @pl.when(j == jj)
        def _(ref=ref):
            ref[...] = proj()


def _inproj(x2, g, w_bf, tm):
    n = x2.shape[0]
    row = lambda i, j: (i, 0)
    row3 = lambda i, j: (i, 0, 0)
    flat_bf = jax.ShapeDtypeStruct((n, PROJ_W), BF16)
    flat_f32 = jax.ShapeDtypeStruct((n, PROJ_W), F32)
    heads = jax.ShapeDtypeStruct((n, N_HEADS, HEAD_W), F32)
    flat_spec = pl.BlockSpec((tm, PROJ_W), row)
    head_spec = pl.BlockSpec((tm, N_HEADS, HEAD_W), row3)
    return pl.pallas_call(
        _inproj_kernel,
        grid=(n // tm, N_PROJ),
        in_specs=[pl.BlockSpec((tm, D_MODEL), row),
                  pl.BlockSpec((1, D_MODEL), lambda i, j: (0, 0)),
                  pl.BlockSpec((D_MODEL, PROJ_W), lambda i, j: (0, j))],
        out_specs=[flat_spec, head_spec, head_spec, flat_spec, flat_spec, flat_spec, flat_spec],
        out_shape=[flat_bf, heads, heads, flat_bf, flat_bf, flat_f32, flat_f32],
        scratch_shapes=[pltpu.VMEM((tm, D_MODEL), BF16)],
        compiler_params=_cparams("parallel", "arbitrary"),
        name="inproj",
    )(x2, g, w_bf)


def _split_maps(q):
    lane = lax.broadcasted_iota(I32, q.shape, 1)
    zero = jnp.zeros_like(q)
    return jnp.concatenate([jnp.where(lane < HEAD_DIM, q, zero),
                            jnp.where(lane >= HEAD_DIM, q, zero)], axis=0)


def _qk(q2, kj):
    return lax.dot_general(q2, kj, (((1,), (1,)), ((), ())), preferred_element_type=F32)


def _prompt_attn_kernel(lam_ref, q_ref, k_ref, v_ref, t0_ref, t1_ref, g_ref, wa_ref, wb_ref,
                        o_ref, wa_out_ref, wb_out_ref, vt_sc, m_sc, acc_sc, *, tile, out_scale):
    i = pl.program_id(2)
    n_kv = vt_sc.shape[0]

    wa_out_ref[...] = wa_ref[...].astype(BF16)
    wb_out_ref[...] = wb_ref[...].astype(BF16)

    @pl.when(i == 0)
    def _():
        ones = jnp.ones((ONES_ROWS, tile), BF16)
        for j in range(n_kv):
            vt = v_ref[j * tile:(j + 1) * tile, :].astype(F32).T.astype(BF16)
            vt_sc[j] = jnp.concatenate([vt, ones], axis=0)

    q_t = _split_maps(q_ref[...]).astype(F32).T.astype(BF16)
    m_sc[...] = jnp.full(m_sc.shape, NEG_INF, F32)
    acc_sc[...] = jnp.zeros(acc_sc.shape, F32)

    def block(first, count, bias_t):
        start = pl.multiple_of(first * tile, tile)
        kj = k_ref[pl.ds(start, count * tile), :]
        s = jnp.dot(kj, q_t, preferred_element_type=F32)
        if bias_t is not None:
            s = s + jnp.concatenate([bias_t, bias_t], axis=1)
        m_prev = m_sc[...]
        m_new = jnp.maximum(m_prev, jnp.max(s, axis=0, keepdims=True))
        alpha = jnp.exp2(m_prev - m_new)
        pb = jnp.exp2(s - m_new).astype(BF16)
        pv = jnp.dot(vt_sc[first], pb[0:tile], preferred_element_type=F32)
        for u in range(1, count):
            pv = pv + jnp.dot(vt_sc[first + u], pb[u * tile:(u + 1) * tile], preferred_element_type=F32)
        acc_sc[...] = alpha * acc_sc[...] + pv
        m_sc[...] = m_new

    n_far = i - 1

    def pair_body(jj, c):
        block(2 * jj, 2, None)
        return c

    lax.fori_loop(0, jnp.maximum(n_far, 0) // 2, pair_body, 0)

    @pl.when(jnp.logical_and(n_far > 0, n_far % 2 == 1))
    def _():
        block(n_far - 1, 1, None)

    @pl.when(i >= 1)
    def _():
        block(i - 1, 1, t1_ref[0])

    block(i, 1, t0_ref[0])

    o = acc_sc[0:HEAD_W, :] / acc_sc[HEAD_W:HEAD_W + 1, :]
    d = o[:, :tile] - lam_ref[0] * o[:, tile:]
    y = d * lax.rsqrt(jnp.mean(d * d, axis=0, keepdims=True) + EPS) * out_scale
    o_ref[...] = (y.T * g_ref[...]).astype(o_ref.dtype)


def _prompt_attention(q, k_bf, v_bf, t0, t1, lam, subln_g, w_a, w_b, batch, seq, out_scale):
    tile = ATTN_TILE
    nq = seq // tile
    n_steps = batch * N_HEADS * nq
    ra, rb = w_a.shape[0] // n_steps, w_b.shape[0] // n_steps
    assert ra * n_steps == w_a.shape[0] and rb * n_steps == w_b.shape[0]
    kern = functools.partial(_prompt_attn_kernel, tile=tile, out_scale=out_scale)
    qmap = lambda b, h, i: (b * nq + i, h)
    kvmap = lambda b, h, i: (b, h)
    tmap = lambda b, h, i: (h, 0, 0)
    wmap = lambda b, h, i: ((b * N_HEADS + h) * nq + i, 0)
    wa_spec = pl.BlockSpec((ra, w_a.shape[1]), wmap)
    wb_spec = pl.BlockSpec((rb, w_b.shape[1]), wmap)
    return pl.pallas_call(
        kern,
        grid=(batch, N_HEADS, nq),
        in_specs=[pl.BlockSpec(memory_space=pltpu.SMEM),
                  pl.BlockSpec((tile, HEAD_W), qmap),
                  pl.BlockSpec((seq, HEAD_W), kvmap),
                  pl.BlockSpec((seq, HEAD_W), kvmap),
                  pl.BlockSpec((1, tile, tile), tmap),
                  pl.BlockSpec((1, tile, tile), tmap),
                  pl.BlockSpec((1, HEAD_W), lambda b, h, i: (0, 0)),
                  wa_spec, wb_spec],
        out_specs=[pl.BlockSpec((tile, HEAD_W), qmap), wa_spec, wb_spec],
        out_shape=[jax.ShapeDtypeStruct((batch * seq, ATTN_WIDTH), BF16),
                   jax.ShapeDtypeStruct(w_a.shape, BF16), jax.ShapeDtypeStruct(w_b.shape, BF16)],
        scratch_shapes=[pltpu.VMEM((nq, HEAD_W + ONES_ROWS, tile), BF16),
                        pltpu.VMEM((1, 2 * tile), F32),
                        pltpu.VMEM((HEAD_W + ONES_ROWS, 2 * tile), F32)],
        compiler_params=_cparams("parallel", "parallel", "arbitrary"),
        name="prompt_attn",
    )(lam, q, k_bf, v_bf, t0, t1, subln_g, w_a, w_b)


def _softmax_update(s_list, v_list, m_sc, l_sc, acc_sc):
    m_prev = m_sc[...]
    m_new = m_prev
    for s in s_list:
        m_new = jnp.maximum(m_new, jnp.max(s, axis=-1, keepdims=True))
    alpha = jnp.exp(m_prev - m_new)
    l_new = alpha * l_sc[...]
    acc = alpha * acc_sc[...]
    for s, vj in zip(s_list, v_list):
        p = jnp.exp(s - m_new)
        l_new = l_new + jnp.sum(p, axis=-1, keepdims=True)
        acc = acc + jnp.dot(p.astype(BF16), vj, preferred_element_type=F32)
    l_sc[...] = l_new
    acc_sc[...] = acc
    m_sc[...] = m_new


def _sample_attn_kernel(pt_ref, lam_ref, q_ref, *refs, n_pp, out_scale):
    k_refs = refs[:n_pp]
    v_refs = refs[n_pp:2 * n_pp]
    tab_ref, ks_ref, vs_ref, tabs_ref, g_ref, o_ref, m_sc, l_sc, acc_sc = refs[2 * n_pp:]
    s_idx = pl.program_id(1)
    cols = k_refs[0].shape[1]

    @pl.when(s_idx == 0)
    def _():
        m_sc[...] = jnp.full(m_sc.shape, NEG_INF, F32)
        l_sc[...] = jnp.zeros(l_sc.shape, F32)
        acc_sc[...] = jnp.zeros(acc_sc.shape, F32)

    q2 = _split_maps(q_ref[0])
    s_list = [_qk(q2, k_refs[pp][0].astype(BF16)) + tab_ref[0, :, pp * cols:(pp + 1) * cols]
              for pp in range(n_pp)]
    _softmax_update(s_list, [v_refs[pp][0].astype(BF16) for pp in range(n_pp)], m_sc, l_sc, acc_sc)

    @pl.when(s_idx == pl.num_programs(1) - 1)
    def _():
        s = _qk(q2, ks_ref[0].astype(BF16)) + tabs_ref[...]
        _softmax_update([s], [vs_ref[0].astype(BF16)], m_sc, l_sc, acc_sc)
        o = acc_sc[...] / l_sc[...]
        r = o.shape[0] // 2
        d = o[:r] - lam_ref[0] * o[r:]
        o_ref[0] = (_rms(d, g_ref[...]) * out_scale).astype(o_ref.dtype)


def _sample_attention(q3, cache_k3, cache_v3, page_table, tab, k_self, v_self, tab_self, lam, subln_g,
                      out_scale):
    nb, rows, _ = q3.shape
    n_pages = page_table.shape[1]
    n_pp = PAGES_PER_STEP
    n_steps = n_pages // n_pp
    cols = cache_k3.shape[1]
    kern = functools.partial(_sample_attn_kernel, n_pp=n_pp, out_scale=out_scale)

    def page_map(pp):
        return lambda b, s, pt: (pt[b, s * n_pp + pp], 0, 0)

    page_specs = [pl.BlockSpec((1, cols, HEAD_W), page_map(pp)) for pp in range(n_pp)]
    bmap = lambda b, s, pt: (b, 0, 0)
    grid_spec = pltpu.PrefetchScalarGridSpec(
        num_scalar_prefetch=1,
        grid=(nb, n_steps),
        in_specs=[pl.BlockSpec(memory_space=pltpu.SMEM),
                  pl.BlockSpec((1, rows, HEAD_W), bmap)]
                 + page_specs + page_specs
                 + [pl.BlockSpec((1, 2 * rows, n_pp * cols),
                                 lambda b, s, pt: (jnp.where(s == n_steps - 1, 1, 0), 0, 0)),
                    pl.BlockSpec((1, HEAD_W, HEAD_W), bmap),
                    pl.BlockSpec((1, HEAD_W, HEAD_W), bmap),
                    pl.BlockSpec((2 * rows, HEAD_W), lambda b, s, pt: (0, 0)),
                    pl.BlockSpec((1, HEAD_W), lambda b, s, pt: (0, 0))],
        out_specs=pl.BlockSpec((1, rows, HEAD_W), bmap),
        scratch_shapes=[pltpu.VMEM((2 * rows, 1), F32), pltpu.VMEM((2 * rows, 1), F32),
                        pltpu.VMEM((2 * rows, HEAD_W), F32)],
    )
    return pl.pallas_call(
        kern,
        grid_spec=grid_spec,
        out_shape=jax.ShapeDtypeStruct((nb, rows, HEAD_W), BF16),
        compiler_params=_cparams("parallel", "arbitrary"),
        name="sample_attn",
    )(page_table, lam, q3, *([cache_k3] * n_pp), *([cache_v3] * n_pp), tab, k_self, v_self, tab_self,
      subln_g)


def _softplus(x):
    return jnp.maximum(x, 0.0) + jnp.log1p(jnp.exp(-jnp.abs(x)))


def _gelu_tanh(x):
    return 0.5 * x * (1.0 + jnp.tanh(math.sqrt(2.0 / math.pi) * (x + 0.044715 * x * x * x)))


def _block_gate(cb, w_ref, b):
    parts = [jnp.dot(cb[:, n * LRU_BLOCK_W:(n + 1) * LRU_BLOCK_W], w_ref[n], preferred_element_type=F32)
             for n in range(LRU_BLOCKS)]
    return jax.nn.sigmoid(jnp.concatenate(parts, axis=-1) + b)


def _lru_coeffs(conv, wrg_ref, brg, wig_ref, big, lam):
    cb = conv.astype(BF16)
    rg = _block_gate(cb, wrg_ref, brg)
    ig = _block_gate(cb, wig_ref, big)
    log_a = -LRU_C * rg * _softplus(-lam)
    a = jnp.exp(log_a)
    bx = jnp.sqrt(jnp.tanh(-log_a) * (1.0 + a * a)) * ig * conv
    return a, bx


def _prompt_lru_kernel(xr_ref, gate_ref, cw_ref, cb_ref, wrg_ref, brg_ref, wig_ref, big_ref, lam_ref,
                       y_ref, conv_ref, h_ref, xc_sc, h_sc, *, tile):
    t = pl.program_id(1)
    pad = SUBLANES

    @pl.when(t == 0)
    def _():
        xc_sc[0:pad, :] = jnp.zeros((pad, LRU_WIDTH), F32)
        h_sc[...] = jnp.zeros(h_sc.shape, F32)

    xr = xr_ref[...]
    xc_sc[pad:pad + tile, :] = xr
    conv = cb_ref[...]
    for j in range(CONV_W):
        off = pad - (CONV_W - 1) + j
        conv = conv + xc_sc[off:off + tile, :] * cw_ref[j:j + 1, :]
    xc_sc[0:pad, :] = xr[tile - pad:, :]

    a, bx = _lru_coeffs(conv, wrg_ref, brg_ref[...], wig_ref, big_ref[...], lam_ref[...])

    row = lax.broadcasted_iota(I32, a.shape, 0)
    d = 1
    while d < tile:
        keep = row >= d
        a_sh = pltpu.roll(a, d, 0)
        b_sh = pltpu.roll(bx, d, 0)
        bx = jnp.where(keep, a * b_sh + bx, bx)
        a = jnp.where(keep, a * a_sh, a)
        d *= 2
    h = bx + a * h_sc[...]
    h_sc[...] = h[tile - 1:tile, :]
    y_ref[...] = (h * _gelu_tanh(gate_ref[...])).astype(y_ref.dtype)

    @pl.when(t == pl.num_programs(1) - 1)
    def _():
        conv_ref[0] = xc_sc[pad + tile - (CONV_W - 1):pad + tile, :]
        h_ref[0] = h[tile - 1:tile, :]


def _prompt_lru(xr, gate, conv_w, conv_b, w_rg, b_rg, w_ig, b_ig, lru_lambda, batch, seq):
    tile = LRU_TILE
    nt = seq // tile
    kern = functools.partial(_prompt_lru_kernel, tile=tile)
    rmap = lambda b, t: (b * nt + t, 0)
    full2 = lambda b, t: (0, 0)
    full3 = lambda b, t: (0, 0, 0)
    bmap = lambda b, t: (b, 0, 0)
    vec = pl.BlockSpec((1, LRU_WIDTH), full2)
    wspec = pl.BlockSpec((LRU_BLOCKS, LRU_BLOCK_W, LRU_BLOCK_W), full3)
    return pl.pallas_call(
        kern,
        grid=(batch, nt),
        in_specs=[pl.BlockSpec((tile, LRU_WIDTH), rmap), pl.BlockSpec((tile, LRU_WIDTH), rmap),
                  pl.BlockSpec((CONV_W, LRU_WIDTH), full2), vec, wspec, vec, wspec, vec, vec],
        out_specs=[pl.BlockSpec((tile, LRU_WIDTH), rmap),
                   pl.BlockSpec((1, CONV_W - 1, LRU_WIDTH), bmap),
                   pl.BlockSpec((1, 1, LRU_WIDTH), bmap)],
        out_shape=[jax.ShapeDtypeStruct((batch * seq, LRU_WIDTH), BF16),
                   jax.ShapeDtypeStruct((batch, CONV_W - 1, LRU_WIDTH), F32),
                   jax.ShapeDtypeStruct((batch, 1, LRU_WIDTH), F32)],
        scratch_shapes=[pltpu.VMEM((tile + SUBLANES, LRU_WIDTH), F32), pltpu.VMEM((1, LRU_WIDTH), F32)],
        compiler_params=_cparams("parallel", "arbitrary"),
        name="prompt_lru",
    )(xr, gate, conv_w, conv_b, w_rg, b_rg, w_ig, b_ig, lru_lambda)


def _sample_lru_kernel(xr_ref, gate_ref, cprev_ref, hprev_ref, cw_ref, cb_ref, wrg_ref, brg_ref,
                       wig_ref, big_ref, lam_ref, y_ref, conv_ref, h_ref, *, steps):
    xcat = [cprev_ref[j] for j in range(CONV_W - 1)] + [xr_ref[t] for t in range(steps)]
    h = hprev_ref[...]
    for t in range(steps):
        conv = cb_ref[...]
        for j in range(CONV_W):
            conv = conv + xcat[t + j] * cw_ref[j:j + 1, :]
        a, bx = _lru_coeffs(conv, wrg_ref, brg_ref[...], wig_ref, big_ref[...], lam_ref[...])
        h = a * h + bx
        y_ref[t] = (h * _gelu_tanh(gate_ref[t])).astype(y_ref.dtype)
    for j in range(CONV_W - 1):
        conv_ref[j] = xcat[steps + j]
    h_ref[...] = h


def _sample_lru(xr_t, gate_t, cprev_t, hprev, conv_w, conv_b, w_rg, b_rg, w_ig, b_ig, lru_lambda):
    steps, nb, _ = xr_t.shape
    kern = functools.partial(_sample_lru_kernel, steps=steps)
    return pl.pallas_call(
        kern,
        out_shape=[jax.ShapeDtypeStruct((steps, nb, LRU_WIDTH), BF16),
                   jax.ShapeDtypeStruct((CONV_W - 1, nb, LRU_WIDTH), F32),
                   jax.ShapeDtypeStruct((nb, LRU_WIDTH), F32)],
        compiler_params=pltpu.CompilerParams(vmem_limit_bytes=VMEM_LIMIT),
        name="sample_lru",
    )(xr_t, gate_t, cprev_t, hprev, conv_w, conv_b, w_rg, b_rg, w_ig, b_ig, lru_lambda)


def _split_bf16(x):
    hi = x.astype(BF16)
    return hi, (x - hi.astype(F32)).astype(BF16)


def _outproj_kernel(a_ref, l_ref, w_ref, x_ref, g_ref, rw_ref, rb_ref, xmid_ref, xp_ref, logit_ref):
    tm = x_ref.shape[0]
    y = jnp.dot(a_ref[...], w_ref[0:ATTN_WIDTH, :], preferred_element_type=F32)
    y = y + jnp.dot(l_ref[...], w_ref[ATTN_WIDTH:, :], preferred_element_type=F32)
    xm = x_ref[...] + y
    xmid_ref[...] = xm
    xn = _rms(xm, g_ref[...])
    xn_hi, xn_lo = _split_bf16(xn)
    bits = _bf16_bits(xn)
    words = bits[:, PACK_W:] | (bits[:, :PACK_W] >> 16)
    xp_ref[...] = words.reshape(tm, PACK_S, LANES)
    rw_hi, rw_lo = _split_bf16(rw_ref[...])
    logits = (jnp.dot(xn_hi, rw_hi, preferred_element_type=F32)
              + jnp.dot(xn_lo, rw_hi, preferred_element_type=F32)
              + jnp.dot(xn_hi, rw_lo, preferred_element_type=F32))
    logit_ref[...] = logits + rb_ref[...]


def _outproj(attn, lru, w_out_bf, x2, g, router_w, router_b, tm):
    n = x2.shape[0]
    row = lambda i: (i, 0)
    full = lambda i: (0, 0)
    return pl.pallas_call(
        _outproj_kernel,
        grid=(n // tm,),
        in_specs=[pl.BlockSpec((tm, ATTN_WIDTH), row), pl.BlockSpec((tm, LRU_WIDTH), row),
                  pl.BlockSpec((D_MODEL, D_MODEL), full, pipeline_mode=pl.Buffered(1)),
                  pl.BlockSpec((tm, D_MODEL), row),
                  pl.BlockSpec((1, D_MODEL), full), pl.BlockSpec((D_MODEL, N_EXPERTS), full),
                  pl.BlockSpec((1, N_EXPERTS), full)],
        out_specs=[pl.BlockSpec((tm, D_MODEL), row), pl.BlockSpec((tm, PACK_S, LANES), lambda i: (i, 0, 0)),
                   pl.BlockSpec((tm, N_EXPERTS), row)],
        out_shape=[jax.ShapeDtypeStruct((n, D_MODEL), F32), jax.ShapeDtypeStruct((n, PACK_S, LANES), U32),
                   jax.ShapeDtypeStruct((n, N_EXPERTS), F32)],
        compiler_params=_cparams("parallel"),
        name="outproj",
    )(attn, lru, w_out_bf, x2, g, router_w, router_b)


def _route_kernel(logit_ref, base_ref, idx_ref, gate_ref, rank_ref, cnt_ref, cnt_sc):
    i = pl.program_id(0)
    tr = logit_ref.shape[0]

    @pl.when(i == 0)
    def _():
        cnt_sc[...] = base_ref[...]

    logits = logit_ref[...]
    lane = lax.broadcasted_iota(I32, logits.shape, 1)
    vals, firsts, hots = [], [], []
    for _ in range(TOP_K):
        mx = jnp.max(logits, axis=-1, keepdims=True)
        first = jnp.min(jnp.where(logits == mx, lane, N_EXPERTS), axis=-1, keepdims=True)
        hot = lane == first
        vals.append(mx)
        firsts.append(first)
        hots.append(hot)
        logits = jnp.where(hot, -jnp.inf, logits)

    member = jnp.zeros(logits.shape, F32)
    for hot in hots:
        member = member + hot.astype(F32)
    r_i = lax.broadcasted_iota(I32, (tr, tr), 0)
    c_i = lax.broadcasted_iota(I32, (tr, tr), 1)
    earlier = jnp.where(c_i < r_i, 1.0, 0.0).astype(BF16)
    before = cnt_sc[...] + jnp.dot(earlier, member.astype(BF16), preferred_element_type=F32)

    exps = [jnp.exp(v - vals[0]) for v in vals]
    denom = exps[0]
    for e in exps[1:]:
        denom = denom + e
    for k in range(TOP_K):
        idx_ref[:, k:k + 1] = firsts[k]
        gate_ref[:, k:k + 1] = exps[k] / denom
        rank = jnp.sum(jnp.where(hots[k], before, 0.0), axis=-1, keepdims=True)
        rank_ref[:, k:k + 1] = rank.astype(I32)
    cnt_sc[...] = cnt_sc[...] + jnp.sum(member, axis=0, keepdims=True)

    @pl.when(i == pl.num_programs(0) - 1)
    def _():
        cnt_ref[...] = cnt_sc[...]


def _route(logits, base_counts, tr):
    n = logits.shape[0]
    row = lambda i: (i, 0)
    full = lambda i: (0, 0)
    k_spec = pl.BlockSpec((tr, TOP_K), row)
    return pl.pallas_call(
        _route_kernel,
        grid=(n // tr,),
        in_specs=[pl.BlockSpec((tr, N_EXPERTS), row), pl.BlockSpec((1, N_EXPERTS), full)],
        out_specs=[k_spec, k_spec, k_spec, pl.BlockSpec((1, N_EXPERTS), full)],
        out_shape=[jax.ShapeDtypeStruct((n, TOP_K), I32), jax.ShapeDtypeStruct((n, TOP_K), F32),
                   jax.ShapeDtypeStruct((n, TOP_K), I32), jax.ShapeDtypeStruct((1, N_EXPERTS), F32)],
        scratch_shapes=[pltpu.VMEM((1, N_EXPERTS), F32)],
        compiler_params=_cparams("arbitrary"),
        name="route",
    )(logits, base_counts)


def _dispatch_kernel(pos_ref, x_ref, xs_in_ref, xs_ref, sem):
    del xs_in_ref
    tm = x_ref.shape[0]

    def issue(r, c):
        for k in range(TOP_K):
            pltpu.make_async_copy(x_ref.at[r], xs_ref.at[pos_ref[r * TOP_K + k]], sem).start(priority=k % 2)
        return c

    lax.fori_loop(0, tm, issue, 0)
    for _ in range(TOP_K):
        pltpu.make_async_copy(x_ref, xs_ref.at[pl.ds(0, tm)], sem).wait()


def _dispatch(pos_flat, x_packed, x_sorted, tm):
    n = x_packed.shape[0]
    return pl.pallas_call(
        _dispatch_kernel,
        grid=(n // tm,),
        in_specs=[pl.BlockSpec((tm * TOP_K,), lambda i: (i,), memory_space=pltpu.SMEM),
                  pl.BlockSpec((tm, PACK_S, LANES), lambda i: (i, 0, 0)),
                  pl.BlockSpec(memory_space=pl.ANY)],
        out_specs=pl.BlockSpec(memory_space=pl.ANY),
        out_shape=jax.ShapeDtypeStruct(x_sorted.shape, x_sorted.dtype),
        scratch_shapes=[pltpu.SemaphoreType.DMA],
        input_output_aliases={2: 0},
        compiler_params=_cparams("arbitrary"),
        name="moe_dispatch",
    )(pos_flat, x_packed, x_sorted)


def _moe_kernel(te_ref, na_ref, x_ref, wg_ref, wu_ref, wd_ref, bg_ref, bu_ref, bd_ref, o_ref,
                xb_sc, act_sc, lo_sc, *, n_f):
    i = pl.program_id(0)
    j = pl.program_id(1)
    tm = x_ref.shape[0]
    tf = act_sc.shape[2]
    last = pl.num_programs(1) - 1
    active = i < na_ref[0]

    @pl.when(jnp.logical_and(active, j == 0))
    def _():
        lo, hi = _unpack_words(x_ref[...].reshape(tm, PACK_W))
        xb_sc[:, :PACK_W] = lo.astype(BF16)
        xb_sc[:, PACK_W:] = hi.astype(BF16)

    @pl.when(jnp.logical_and(active, j < n_f))
    def _():
        x = xb_sc[...]
        glu = jnp.dot(x, wg_ref[0], preferred_element_type=F32) + bg_ref[0]
        lin = jnp.dot(x, wu_ref[0], preferred_element_type=F32) + bu_ref[0]
        glu = jnp.minimum(glu, SWIGLU_LIMIT)
        lin = jnp.clip(lin, -SWIGLU_LIMIT, SWIGLU_LIMIT)
        act_sc[j] = (glu * jax.nn.sigmoid(SWIGLU_ALPHA * glu) * (lin + 1.0)).astype(BF16)

    @pl.when(jnp.logical_and(active, j >= n_f))
    def _():
        y = bd_ref[0] + jnp.dot(act_sc[0], wd_ref[0, 0:tf, :], preferred_element_type=F32)
        for f in range(1, n_f):
            y = y + jnp.dot(act_sc[f], wd_ref[0, f * tf:(f + 1) * tf, :], preferred_element_type=F32)
        bits = _bf16_bits(y)

        @pl.when(j == n_f)
        def _():
            lo_sc[...] = bits >> 16

        @pl.when(j == last)
        def _():
            o_ref[...] = (bits | lo_sc[...]).reshape(tm, PACK_S, LANES)

    @pl.when(jnp.logical_and(jnp.logical_not(active), j == last))
    def _():
        o_ref[...] = jnp.zeros(o_ref.shape, U32)


def _moe_ffn(x_sorted, tile_expert, n_active, w_gate_up, b_gate_up, w_down, b_down):
    n_slots = x_sorted.shape[0]
    tm, tf = MOE_TILE, MOE_FF_TILE
    n_tiles = n_slots // tm
    n_f = D_FF // tf
    n_d = D_MODEL // PACK_W
    kern = functools.partial(_moe_kernel, n_f=n_f)

    def fidx(i, j, na):
        return jnp.where(i < na[0], jnp.minimum(j, n_f - 1), n_f - 1)

    def down_block(i, j, te, na):
        in_down = jnp.logical_and(i < na[0], j >= n_f)
        e = jnp.where(in_down, te[i], te[jnp.maximum(i - 1, 0)])
        return e, 0, jnp.where(in_down, j - n_f, n_d - 1)

    def rowidx(i, na):
        return jnp.minimum(i, na[0] - 1)

    grid_spec = pltpu.PrefetchScalarGridSpec(
        num_scalar_prefetch=2,
        grid=(n_tiles, n_f + n_d),
        in_specs=[pl.BlockSpec((tm, PACK_S, LANES), lambda i, j, te, na: (rowidx(i, na), 0, 0)),
                  pl.BlockSpec((1, D_MODEL, tf), lambda i, j, te, na: (te[i], 0, fidx(i, j, na))),
                  pl.BlockSpec((1, D_MODEL, tf), lambda i, j, te, na: (te[i], 0, n_f + fidx(i, j, na))),
                  pl.BlockSpec((1, D_FF, PACK_W), down_block),
                  pl.BlockSpec((1, 1, tf), lambda i, j, te, na: (te[i], 0, fidx(i, j, na))),
                  pl.BlockSpec((1, 1, tf), lambda i, j, te, na: (te[i], 0, n_f + fidx(i, j, na))),
                  pl.BlockSpec((1, 1, PACK_W), down_block)],
        out_specs=pl.BlockSpec((tm, PACK_S, LANES), lambda i, j, te, na: (i, 0, 0)),
        scratch_shapes=[pltpu.VMEM((tm, D_MODEL), BF16), pltpu.VMEM((n_f, tm, tf), BF16),
                        pltpu.VMEM((tm, PACK_W), U32)],
    )
    bgu3 = b_gate_up.reshape(N_EXPERTS, 1, 2 * D_FF)
    bd3 = b_down.reshape(N_EXPERTS, 1, D_MODEL)
    return pl.pallas_call(
        kern,
        grid_spec=grid_spec,
        out_shape=jax.ShapeDtypeStruct((n_slots, PACK_S, LANES), U32),
        compiler_params=_cparams("arbitrary", "arbitrary"),
        name="moe_ffn",
    )(tile_expert, n_active, x_sorted, w_gate_up, w_gate_up, w_down, bgu3, bgu3, bd3)


def _combine_kernel(pos_ref, pos_next_ref, gate_ref, x_ref, g_ref, y_hbm, o_ref, ybuf, sems):
    i = pl.program_id(0)
    n = pl.num_programs(0)
    tm = x_ref.shape[0]
    slot = i % 2
    rows_per_group = min(COMBINE_GROUP, tm)

    def issue_group(p_ref, buf_slot, r0):
        for rr in range(rows_per_group):
            r = r0 + rr
            for k in range(TOP_K):
                pltpu.make_async_copy(y_hbm.at[p_ref[r * TOP_K + k]], ybuf.at[buf_slot, k, r],
                                      sems.at[buf_slot]).start()

    def wait_slot(buf_slot):
        for k in range(TOP_K):
            pltpu.make_async_copy(y_hbm.at[pl.ds(0, tm)], ybuf.at[buf_slot, k], sems.at[buf_slot]).wait()

    @pl.when(i == 0)
    def _():
        def first(gi, c):
            issue_group(pos_ref, 0, gi * rows_per_group)
            return c
        lax.fori_loop(0, tm // rows_per_group, first, 0)

    wait_slot(slot)

    def group(gi, c):
        r0 = pl.multiple_of(gi * rows_per_group, rows_per_group)
        issue_group(pos_next_ref, 1 - slot, r0)
        rows = pl.ds(r0, rows_per_group)
        gates = gate_ref[rows, :]
        x = x_ref[rows, :]
        out_lo, out_hi = x[:, :PACK_W], x[:, PACK_W:]
        for k in range(TOP_K):
            lo, hi = _unpack_words(ybuf[slot, k, rows].reshape(rows_per_group, PACK_W))
            out_lo = out_lo + gates[:, k:k + 1] * lo
            out_hi = out_hi + gates[:, k:k + 1] * hi
        o_ref[rows, :] = _rms(jnp.concatenate([out_lo, out_hi], axis=1), g_ref[...])
        return c

    lax.fori_loop(0, tm // rows_per_group, group, 0)

    @pl.when(i == n - 1)
    def _():
        wait_slot(1 - slot)


def _combine(pos_flat, gates, x_mid, g, y_sorted, tm):
    n = x_mid.shape[0]
    steps = n // tm
    row = lambda i: (i, 0)
    return pl.pallas_call(
        _combine_kernel,
        grid=(steps,),
        in_specs=[pl.BlockSpec((tm * TOP_K,), lambda i: (i,), memory_space=pltpu.SMEM),
                  pl.BlockSpec((tm * TOP_K,), lambda i: (jnp.minimum(i + 1, steps - 1),),
                               memory_space=pltpu.SMEM),
                  pl.BlockSpec((tm, TOP_K), row),
                  pl.BlockSpec((tm, D_MODEL), row),
                  pl.BlockSpec((1, D_MODEL), lambda i: (0, 0)),
                  pl.BlockSpec(memory_space=pl.ANY)],
        out_specs=pl.BlockSpec((tm, D_MODEL), row),
        out_shape=jax.ShapeDtypeStruct((n, D_MODEL), F32),
        scratch_shapes=[pltpu.VMEM((2, TOP_K, tm, PACK_S, LANES), U32), pltpu.SemaphoreType.DMA((2,))],
        compiler_params=_cparams("arbitrary"),
        name="moe_combine",
    )(pos_flat, pos_flat, gates, x_mid, g, y_sorted)


def _bias_of_distance(rel_table, n):
    nf = jnp.maximum(n, 1).astype(F32)
    large = REL_EXACT + (jnp.log(nf / REL_EXACT) / math.log(REL_MAX_DIST / REL_EXACT)
                         * (REL_BUCKETS - REL_EXACT)).astype(I32)
    bucket = jnp.where(n < REL_EXACT, n, jnp.minimum(large, REL_BUCKETS - 1))
    hot = bucket[..., None, None] == jnp.arange(REL_BUCKETS, dtype=I32)[:, None]
    return jnp.sum(jnp.where(hot, rel_table.astype(F32), 0.0), axis=-2)


def _prompt_bias_tables(rel_table, tile):
    far = rel_table[REL_BUCKETS - 1].astype(F32)
    d = jnp.arange(tile)[None, :] - jnp.arange(tile)[:, None]
    diag = jnp.where((d >= 0)[..., None],
                     (_bias_of_distance(rel_table, jnp.maximum(d, 0)) - far) * LOG2E, NEG_INF)
    near = (_bias_of_distance(rel_table, d + tile) - far) * LOG2E
    return jnp.transpose(diag, (2, 0, 1)), jnp.transpose(near, (2, 0, 1))


def _sample_bias_tables(rel_table, dec_seq, past_len, page_size):
    heads = jnp.arange(N_HEADS)
    same_head = heads[:, None] == heads[None, :]

    def table(dist, valid):
        b = _bias_of_distance(rel_table, jnp.maximum(dist, 0))
        b = jnp.transpose(b, (0, 2, 1))
        full = jnp.where(same_head[None, :, None, :] & valid[:, None, :, None],
                         b[:, :, :, None], NEG_INF)
        full = full.reshape(dec_seq * N_HEADS, -1)
        return jnp.concatenate([full, full], axis=0)

    qpos = past_len + jnp.arange(dec_seq)
    span = PAGES_PER_STEP * page_size
    always = jnp.ones((dec_seq, span), bool)
    k_last = past_len - span + jnp.arange(span)
    tab_far = table(jnp.full((dec_seq, span), REL_MAX_DIST), always)
    tab_last = table(qpos[:, None] - k_last[None, :], always)
    n_self = HEAD_W // N_HEADS
    tk = jnp.arange(n_self)
    dist = jnp.arange(dec_seq)[:, None] - tk[None, :]
    tab_self = table(dist, (dist >= 0) & (tk[None, :] < dec_seq))
    return jnp.stack([tab_far, tab_last]), tab_self


def _tile_plan(counts, n_assign, tm):
    counts = counts.reshape(N_EXPERTS).astype(I32)
    padded = (counts + tm - 1) // tm * tm
    pad_end = jnp.cumsum(padded)
    pad_start = pad_end - padded
    n_tiles = n_assign // tm + N_EXPERTS
    n_active = (pad_end[-1] // tm).astype(I32)
    tile_start = jnp.minimum(jnp.arange(n_tiles, dtype=I32), n_active - 1) * tm
    tile_expert = jnp.minimum(jnp.sum((pad_end[None, :] <= tile_start[:, None]).astype(I32), axis=1),
                              N_EXPERTS - 1)
    return pad_start, tile_expert, n_active.reshape(1), n_tiles


def _slots(pad_start, idx, rank):
    cols = LANES if idx.size % LANES == 0 else idx.size
    flat = idx.reshape(-1, cols)
    pos = rank.reshape(-1, cols)
    for e in range(N_EXPERTS):
        pos = pos + jnp.where(flat == e, pad_start[e], 0)
    return pos.reshape(-1)


def kernel(x_prompt, x_sample, cache_k, cache_v, state_conv, state_h, page_table, rel_bias, norm_mix_g, w_in, lam_q1, lam_k1, lam_q2, lam_k2, subln_g, conv_w, conv_b, w_rg, b_rg, w_ig, b_ig, lru_lambda, w_out, norm_ffn_g, router_w, router_b, w_gate_up, b_gate_up, w_down, b_down, norm_final_g):
    depth = w_in.shape[0]
    assert depth == 1
    layer = 0
    bp, seq, _ = x_prompt.shape
    bs, dec_seq, _ = x_sample.shape
    n_pool, page_size = cache_k.shape[1], cache_k.shape[2]
    past_len = page_table.shape[1] * page_size
    n_p, n_s = bp * seq, bs * dec_seq
    assert ATTN_TILE >= REL_MAX_DIST and PAGES_PER_STEP * page_size >= REL_MAX_DIST

    lam_init = 0.8 - 0.6 * math.exp(-0.3 * layer)
    out_scale = 1.0 - lam_init
    lam = (jnp.exp(jnp.sum(lam_q1[layer] * lam_k1[layer])) - jnp.exp(jnp.sum(lam_q2[layer] * lam_k2[layer]))
           + lam_init).reshape(1).astype(F32)

    g_mix = norm_mix_g[layer].reshape(1, D_MODEL)
    g_ffn = norm_ffn_g[layer].reshape(1, D_MODEL)
    g_fin = norm_final_g.reshape(1, D_MODEL)
    g_sub = subln_g[layer].reshape(1, HEAD_W)
    w_in_bf = w_in[layer].astype(BF16)
    w_out_bf = w_out[layer].astype(BF16)
    lru_args = (conv_w[layer], conv_b[layer].reshape(1, LRU_WIDTH),
                w_rg[layer].astype(BF16), b_rg[layer].reshape(1, LRU_WIDTH),
                w_ig[layer].astype(BF16), b_ig[layer].reshape(1, LRU_WIDTH),
                lru_lambda[layer].reshape(1, LRU_WIDTH))
    rw = router_w[layer]
    rb = router_b[layer].reshape(1, N_EXPERTS)

    xp2 = x_prompt.reshape(n_p, D_MODEL)
    xs2 = x_sample.reshape(n_s, D_MODEL)

    q_p, k_p, v_p, kb_p, vb_p, xr_p, gate_p = _inproj(xp2, g_mix, w_in_bf, ROW_TILE)
    t0, t1 = _prompt_bias_tables(rel_bias, ATTN_TILE)
    attn_p, wgu_bf, wd_bf = _prompt_attention(
        q_p, kb_p, vb_p, t0, t1, lam, g_sub,
        w_gate_up[layer].reshape(N_EXPERTS * D_MODEL, 2 * D_FF), w_down[layer].reshape(N_EXPERTS * D_FF, D_MODEL),
        bp, seq, out_scale)
    lru_p, conv_p, h_p = _prompt_lru(xr_p, gate_p, *lru_args, bp, seq)
    xmid_p, xpk_p, logit_p = _outproj(attn_p, lru_p, w_out_bf, xp2, g_ffn, rw, rb, ROW_TILE)

    q_s, k_s, v_s, _, _, xr_s, gate_s = _inproj(xs2, g_mix, w_in_bf, n_s)
    tab, tab_self = _sample_bias_tables(rel_bias, dec_seq, past_len, page_size)
    rows = dec_seq * N_HEADS
    pad_rows = HEAD_W - rows

    def self_rows(a):
        a3 = a.reshape(bs, rows, HEAD_W)
        return jnp.concatenate([a3, jnp.zeros((bs, pad_rows, HEAD_W), a.dtype)], axis=1)

    attn_s = _sample_attention(
        q_s.reshape(bs, rows, HEAD_W),
        cache_k[layer].reshape(n_pool, page_size * N_HEADS, HEAD_W),
        cache_v[layer].reshape(n_pool, page_size * N_HEADS, HEAD_W),
        page_table, tab, self_rows(k_s), self_rows(v_s), tab_self, lam, g_sub, out_scale)
    attn_s = attn_s.reshape(n_s, ATTN_WIDTH)

    def tmajor(a):
        return jnp.transpose(a.reshape(bs, dec_seq, LRU_WIDTH), (1, 0, 2))

    lru_s_t, conv_s_t, h_s = _sample_lru(tmajor(xr_s), tmajor(gate_s),
                                         jnp.transpose(state_conv[layer], (1, 0, 2)), state_h[layer],
                                         *lru_args)
    lru_s = jnp.transpose(lru_s_t, (1, 0, 2)).reshape(n_s, LRU_WIDTH)
    conv_s = jnp.transpose(conv_s_t, (1, 0, 2))
    xmid_s, xpk_s, logit_s = _outproj(attn_s, lru_s, w_out_bf, xs2, g_ffn, rw, rb, n_s)

    idx_p, gates_p, rank_p, cnt_p = _route(logit_p, jnp.zeros((1, N_EXPERTS), F32), ROW_TILE)
    idx_s, gates_s, rank_s, cnt = _route(logit_s, cnt_p, n_s)
    n_assign = (n_p + n_s) * TOP_K
    pad_start, tile_expert, n_active, n_tiles = _tile_plan(cnt, n_assign, MOE_TILE)
    pos_p = _slots(pad_start, idx_p, rank_p)
    pos_s = _slots(pad_start, idx_s, rank_s)
    x_sorted = jnp.zeros((n_tiles * MOE_TILE, PACK_S, LANES), U32)
    x_sorted = _dispatch(pos_p, xpk_p, x_sorted, ROW_TILE)
    x_sorted = _dispatch(pos_s, xpk_s, x_sorted, n_s)
    y_sorted = _moe_ffn(x_sorted, tile_expert, n_active, wgu_bf.reshape(N_EXPERTS, D_MODEL, 2 * D_FF),
                        b_gate_up[layer], wd_bf.reshape(N_EXPERTS, D_FF, D_MODEL), b_down[layer])
    y_p = _combine(pos_p, gates_p, xmid_p, g_fin, y_sorted, COMBINE_TILE)
    y_s = _combine(pos_s, gates_s, xmid_s, g_fin, y_sorted, n_s)

    kv_p = (depth, bp, seq, N_HEADS, HEAD_W)
    kv_s = (depth, bs, dec_seq, N_HEADS, HEAD_W)
    return (y_p.reshape(bp, seq, D_MODEL), y_s.reshape(bs, dec_seq, D_MODEL),
            k_p.reshape(kv_p), v_p.reshape(kv_p),
            conv_p.reshape(depth, bp, CONV_W - 1, LRU_WIDTH), h_p.reshape(depth, bp, LRU_WIDTH),
            k_s.reshape(kv_s), v_s.reshape(kv_s),
            conv_s.reshape(depth, bs, CONV_W - 1, LRU_WIDTH), h_s.reshape(depth, bs, LRU_WIDTH))
```

```python
import functools
import math

import jax
import jax.numpy as jnp
from jax import lax
from jax.experimental import pallas as pl
from jax.experimental.pallas import tpu as pltpu

F32 = jnp.float32
BF16 = jnp.bfloat16
U32 = jnp.uint32
I32 = jnp.int32

D_MODEL = 2048
ATTN_WIDTH = D_MODEL // 2
LRU_WIDTH = D_MODEL - ATTN_WIDTH
HEAD_DIM = 64
HEAD_W = 2 * HEAD_DIM
N_HEADS = ATTN_WIDTH // HEAD_W
LRU_BLOCKS = 8
LRU_BLOCK_W = LRU_WIDTH // LRU_BLOCKS
CONV_W = 4
LRU_C = 8.0
REL_BUCKETS = 32
REL_EXACT = REL_BUCKETS // 2
REL_MAX_DIST = 128
N_EXPERTS = 32
TOP_K = 4
D_FF = D_MODEL
SWIGLU_LIMIT = 7.0
SWIGLU_ALPHA = 1.702
EPS = 1e-5
NEG_INF = -1e30
LOG2E = math.log2(math.e)
ONES_ROWS = 16
N_PROJ = 5
PROJ_W = 1024

LANES = 128
SUBLANES = 8
VMEM_LIMIT = 56 * 1024 * 1024
PACK_W = D_MODEL // 2
PACK_S = PACK_W // LANES
ROW_S = D_MODEL // LANES

ATTN_TILE = 512
LRU_TILE = 256
ROW_TILE = 512
MOE_TILE = 512
MOE_FF_TILE = 512
COMBINE_TILE = 512
COMBINE_GROUP = 32
PAGES_PER_STEP = 8


def _cparams(*sem):
    return pltpu.CompilerParams(dimension_semantics=sem, vmem_limit_bytes=VMEM_LIMIT)


def _rms(x, g):
    return x * lax.rsqrt(jnp.mean(x * x, axis=-1, keepdims=True) + EPS) * g


def _bf16_bits(x):
    return pltpu.bitcast(x.astype(BF16).astype(F32), U32)


def _unpack_words(w):
    return (pltpu.bitcast(w << 16, F32), pltpu.bitcast(w & jnp.uint32(0xFFFF0000), F32))


def _inproj_kernel(x_ref, g_ref, w_ref, q_ref, kr_ref, vr_ref, kb_ref, vb_ref, xr_ref, gate_ref, xn_sc):
    j = pl.program_id(1)
    tm = x_ref.shape[0]

    @pl.when(j == 0)
    def _():
        xn_sc[...] = _rms(x_ref[...], g_ref[...]).astype(BF16)

    def proj():
        return jnp.dot(xn_sc[...], w_ref[...], preferred_element_type=F32)

    @pl.when(j == 0)
    def _():
        q_ref[...] = (proj() * (HEAD_DIM ** -0.5)).astype(BF16)

    for jj, rows_ref, bf_ref, bf_scale in ((1, kr_ref, kb_ref, LOG2E), (2, vr_ref, vb_ref, None)):
        @pl.when(j == jj)
        def _(rows_ref=rows_ref, bf_ref=bf_ref, bf_scale=bf_scale):
            acc = proj()
            rows_ref[...] = acc.reshape(tm, N_HEADS, HEAD_W)
            bf_ref[...] = (acc if bf_scale is None else acc * bf_scale).astype(BF16)

    for jj, ref in ((3, xr_ref), (4, gate_ref)):
        @pl.when(j == jj)
        def _(ref=ref):
            ref[...] = proj()


def _inproj(x2, g, w_bf, tm):
    n = x2.shape[0]
    row = lambda i, j: (i, 0)
    row3 = lambda i, j: (i, 0, 0)
    flat_bf = jax.ShapeDtypeStruct((n, PROJ_W), BF16)
    flat_f32 = jax.ShapeDtypeStruct((n, PROJ_W), F32)
    heads = jax.ShapeDtypeStruct((n, N_HEADS, HEAD_W), F32)
    flat_spec = pl.BlockSpec((tm, PROJ_W), row)
    head_spec = pl.BlockSpec((tm, N_HEADS, HEAD_W), row3)
    return pl.pallas_call(
        _inproj_kernel,
        grid=(n // tm, N_PROJ),
        in_specs=[pl.BlockSpec((tm, D_MODEL), row),
                  pl.BlockSpec((1, D_MODEL), lambda i, j: (0, 0)),
                  pl.BlockSpec((D_MODEL, PROJ_W), lambda i, j: (0, j))],
        out_specs=[flat_spec, head_spec, head_spec, flat_spec, flat_spec, flat_spec, flat_spec],
        out_shape=[flat_bf, heads, heads, flat_bf, flat_bf, flat_f32, flat_f32],
        scratch_shapes=[pltpu.VMEM((tm, D_MODEL), BF16)],
        compiler_params=_cparams("parallel", "arbitrary"),
        name="inproj",
    )(x2, g, w_bf)


def _split_maps(q):
    lane = lax.broadcasted_iota(I32, q.shape, 1)
    zero = jnp.zeros_like(q)
    return jnp.concatenate([jnp.where(lane < HEAD_DIM, q, zero),
                            jnp.where(lane >= HEAD_DIM, q, zero)], axis=0)


def _qk(q2, kj):
    return lax.dot_general(q2, kj, (((1,), (1,)), ((), ())), preferred_element_type=F32)


def _prompt_attn_kernel(lam_ref, q_ref, k_ref, v_ref, t0_ref, t1_ref, g_ref, wa_ref, wb_ref,
                        o_ref, wa_out_ref, wb_out_ref, vt_sc, m_sc, acc_sc, *, tile, out_scale):
    i = pl.program_id(2)
    n_kv = vt_sc.shape[0]

    wa_out_ref[...] = wa_ref[...].astype(BF16)
    wb_out_ref[...] = wb_ref[...].astype(BF16)

    @pl.when(i == 0)
    def _():
        ones = jnp.ones((ONES_ROWS, tile), BF16)
        for j in range(n_kv):
            vt = v_ref[j * tile:(j + 1) * tile, :].astype(F32).T.astype(BF16)
            vt_sc[j] = jnp.concatenate([vt, ones], axis=0)

    q_t = _split_maps(q_ref[...]).astype(F32).T.astype(BF16)
    m_sc[...] = jnp.full(m_sc.shape, NEG_INF, F32)
    acc_sc[...] = jnp.zeros(acc_sc.shape, F32)

    def block(first, count, bias_t):
        start = pl.multiple_of(first * tile, tile)
        kj = k_ref[pl.ds(start, count * tile), :]
        s = jnp.dot(kj, q_t, preferred_element_type=F32)
        if bias_t is not None:
            s = s + jnp.concatenate([bias_t, bias_t], axis=1)
        m_prev = m_sc[...]
        m_new = jnp.maximum(m_prev, jnp.max(s, axis=0, keepdims=True))
        alpha = jnp.exp2(m_prev - m_new)
        pb = jnp.exp2(s - m_new).astype(BF16)
        pv = jnp.dot(vt_sc[first], pb[0:tile], preferred_element_type=F32)
        for u in range(1, count):
            pv = pv + jnp.dot(vt_sc[first + u], pb[u * tile:(u + 1) * tile], preferred_element_type=F32)
        acc_sc[...] = alpha * acc_sc[...] + pv
        m_sc[...] = m_new

    n_far = i - 1

    def pair_body(jj, c):
        block(2 * jj, 2, None)
        return c

    lax.fori_loop(0, jnp.maximum(n_far, 0) // 2, pair_body, 0)

    @pl.when(jnp.logical_and(n_far > 0, n_far % 2 == 1))
    def _():
        block(n_far - 1, 1, None)

    @pl.when(i >= 1)
    def _():
        block(i - 1, 1, t1_ref[0])

    block(i, 1, t0_ref[0])

    o = acc_sc[0:HEAD_W, :] / acc_sc[HEAD_W:HEAD_W + 1, :]
    d = o[:, :tile] - lam_ref[0] * o[:, tile:]
    y = d * lax.rsqrt(jnp.mean(d * d, axis=0, keepdims=True) + EPS) * out_scale
    o_ref[...] = (y.T * g_ref[...]).astype(o_ref.dtype)


def _prompt_attention(q, k_bf, v_bf, t0, t1, lam, subln_g, w_a, w_b, batch, seq, out_scale):
    tile = ATTN_TILE
    nq = seq // tile
    n_steps = batch * N_HEADS * nq
    ra, rb = w_a.shape[0] // n_steps, w_b.shape[0] // n_steps
    assert ra * n_steps == w_a.shape[0] and rb * n_steps == w_b.shape[0]
    kern = functools.partial(_prompt_attn_kernel, tile=tile, out_scale=out_scale)
    qmap = lambda b, h, i: (b * nq + i, h)
    kvmap = lambda b, h, i: (b, h)
    tmap = lambda b, h, i: (h, 0, 0)
    wmap = lambda b, h, i: ((b * N_HEADS + h) * nq + i, 0)
    wa_spec = pl.BlockSpec((ra, w_a.shape[1]), wmap)
    wb_spec = pl.BlockSpec((rb, w_b.shape[1]), wmap)
    return pl.pallas_call(
        kern,
        grid=(batch, N_HEADS, nq),
        in_specs=[pl.BlockSpec(memory_space=pltpu.SMEM),
                  pl.BlockSpec((tile, HEAD_W), qmap),
                  pl.BlockSpec((seq, HEAD_W), kvmap),
                  pl.BlockSpec((seq, HEAD_W), kvmap),
                  pl.BlockSpec((1, tile, tile), tmap),
                  pl.BlockSpec((1, tile, tile), tmap),
                  pl.BlockSpec((1, HEAD_W), lambda b, h, i: (0, 0)),
                  wa_spec, wb_spec],
        out_specs=[pl.BlockSpec((tile, HEAD_W), qmap), wa_spec, wb_spec],
        out_shape=[jax.ShapeDtypeStruct((batch * seq, ATTN_WIDTH), BF16),
                   jax.ShapeDtypeStruct(w_a.shape, BF16), jax.ShapeDtypeStruct(w_b.shape, BF16)],
        scratch_shapes=[pltpu.VMEM((nq, HEAD_W + ONES_ROWS, tile), BF16),
                        pltpu.VMEM((1, 2 * tile), F32),
                        pltpu.VMEM((HEAD_W + ONES_ROWS, 2 * tile), F32)],
        compiler_params=_cparams("parallel", "parallel", "arbitrary"),
        name="prompt_attn",
    )(lam, q, k_bf, v_bf, t0, t1, subln_g, w_a, w_b)


def _softmax_update(s_list, v_list, m_sc, l_sc, acc_sc):
    m_prev = m_sc[...]
    m_new = m_prev
    for s in s_list:
        m_new = jnp.maximum(m_new, jnp.max(s, axis=-1, keepdims=True))
    alpha = jnp.exp(m_prev - m_new)
    l_new = alpha * l_sc[...]
    acc = alpha * acc_sc[...]
    for s, vj in zip(s_list, v_list):
        p = jnp.exp(s - m_new)
        l_new = l_new + jnp.sum(p, axis=-1, keepdims=True)
        acc = acc + jnp.dot(p.astype(BF16), vj, preferred_element_type=F32)
    l_sc[...] = l_new
    acc_sc[...] = acc
    m_sc[...] = m_new


def _sample_attn_kernel(pt_ref, lam_ref, q_ref, *refs, n_pp, out_scale):
    k_refs = refs[:n_pp]
    v_refs = refs[n_pp:2 * n_pp]
    tab_ref, ks_ref, vs_ref, tabs_ref, g_ref, o_ref, m_sc, l_sc, acc_sc = refs[2 * n_pp:]
    s_idx = pl.program_id(1)
    cols = k_refs[0].shape[1]

    @pl.when(s_idx == 0)
    def _():
        m_sc[...] = jnp.full(m_sc.shape, NEG_INF, F32)
        l_sc[...] = jnp.zeros(l_sc.shape, F32)
        acc_sc[...] = jnp.zeros(acc_sc.shape, F32)

    q2 = _split_maps(q_ref[0])
    s_list = [_qk(q2, k_refs[pp][0].astype(BF16)) + tab_ref[0, :, pp * cols:(pp + 1) * cols]
              for pp in range(n_pp)]
    _softmax_update(s_list, [v_refs[pp][0].astype(BF16) for pp in range(n_pp)], m_sc, l_sc, acc_sc)

    @pl.when(s_idx == pl.num_programs(1) - 1)
    def _():
        s = _qk(q2, ks_ref[0].astype(BF16)) + tabs_ref[...]
        _softmax_update([s], [vs_ref[0].astype(BF16)], m_sc, l_sc, acc_sc)
        o = acc_sc[...] / l_sc[...]
        r = o.shape[0] // 2
        d = o[:r] - lam_ref[0] * o[r:]
        o_ref[0] = (_rms(d, g_ref[...]) * out_scale).astype(o_ref.dtype)


def _sample_attention(q3, cache_k3, cache_v3, page_table, tab, k_self, v_self, tab_self, lam, subln_g,
                      out_scale):
    nb, rows, _ = q3.shape
    n_pages = page_table.shape[1]
    n_pp = PAGES_PER_STEP
    n_steps = n_pages // n_pp
    cols = cache_k3.shape[1]
    kern = functools.partial(_sample_attn_kernel, n_pp=n_pp, out_scale=out_scale)

    def page_map(pp):
        return lambda b, s, pt: (pt[b, s * n_pp + pp], 0, 0)

    page_specs = [pl.BlockSpec((1, cols, HEAD_W), page_map(pp)) for pp in range(n_pp)]
    bmap = lambda b, s, pt: (b, 0, 0)
    grid_spec = pltpu.PrefetchScalarGridSpec(
        num_scalar_prefetch=1,
        grid=(nb, n_steps),
        in_specs=[pl.BlockSpec(memory_space=pltpu.SMEM),
                  pl.BlockSpec((1, rows, HEAD_W), bmap)]
                 + page_specs + page_specs
                 + [pl.BlockSpec((1, 2 * rows, n_pp * cols),
                                 lambda b, s, pt: (jnp.where(s == n_steps - 1, 1, 0), 0, 0)),
                    pl.BlockSpec((1, HEAD_W, HEAD_W), bmap),
                    pl.BlockSpec((1, HEAD_W, HEAD_W), bmap),
                    pl.BlockSpec((2 * rows, HEAD_W), lambda b, s, pt: (0, 0)),
                    pl.BlockSpec((1, HEAD_W), lambda b, s, pt: (0, 0))],
        out_specs=pl.BlockSpec((1, rows, HEAD_W), bmap),
        scratch_shapes=[pltpu.VMEM((2 * rows, 1), F32), pltpu.VMEM((2 * rows, 1), F32),
                        pltpu.VMEM((2 * rows, HEAD_W), F32)],
    )
    return pl.pallas_call(
        kern,
        grid_spec=grid_spec,
        out_shape=jax.ShapeDtypeStruct((nb, rows, HEAD_W), BF16),
        compiler_params=_cparams("parallel", "arbitrary"),
        name="sample_attn",
    )(page_table, lam, q3, *([cache_k3] * n_pp), *([cache_v3] * n_pp), tab, k_self, v_self, tab_self,
      subln_g)


def _softplus(x):
    return jnp.maximum(x, 0.0) + jnp.log1p(jnp.exp(-jnp.abs(x)))


def _gelu_tanh(x):
    return 0.5 * x * (1.0 + jnp.tanh(math.sqrt(2.0 / math.pi) * (x + 0.044715 * x * x * x)))


def _block_gate(cb, w_ref, b):
    parts = [jnp.dot(cb[:, n * LRU_BLOCK_W:(n + 1) * LRU_BLOCK_W], w_ref[n], preferred_element_type=F32)
             for n in range(LRU_BLOCKS)]
    return jax.nn.sigmoid(jnp.concatenate(parts, axis=-1) + b)


def _lru_coeffs(conv, wrg_ref, brg, wig_ref, big, lam):
    cb = conv.astype(BF16)
    rg = _block_gate(cb, wrg_ref, brg)
    ig = _block_gate(cb, wig_ref, big)
    log_a = -LRU_C * rg * _softplus(-lam)
    a = jnp.exp(log_a)
    bx = jnp.sqrt(jnp.tanh(-log_a) * (1.0 + a * a)) * ig * conv
    return a, bx


def _prompt_lru_kernel(xr_ref, gate_ref, cw_ref, cb_ref, wrg_ref, brg_ref, wig_ref, big_ref, lam_ref,
                       y_ref, conv_ref, h_ref, xc_sc, h_sc, *, tile):
    t = pl.program_id(1)
    pad = SUBLANES

    @pl.when(t == 0)
    def _():
        xc_sc[0:pad, :] = jnp.zeros((pad, LRU_WIDTH), F32)
        h_sc[...] = jnp.zeros(h_sc.shape, F32)

    xr = xr_ref[...]
    xc_sc[pad:pad + tile, :] = xr
    conv = cb_ref[...]
    for j in range(CONV_W):
        off = pad - (CONV_W - 1) + j
        conv = conv + xc_sc[off:off + tile, :] * cw_ref[j:j + 1, :]
    xc_sc[0:pad, :] = xr[tile - pad:, :]

    a, bx = _lru_coeffs(conv, wrg_ref, brg_ref[...], wig_ref, big_ref[...], lam_ref[...])

    row = lax.broadcasted_iota(I32, a.shape, 0)
    d = 1
    while d < tile:
        keep = row >= d
        a_sh = pltpu.roll(a, d, 0)
        b_sh = pltpu.roll(bx, d, 0)
        bx = jnp.where(keep, a * b_sh + bx, bx)
        a = jnp.where(keep, a * a_sh, a)
        d *= 2
    h = bx + a * h_sc[...]
    h_sc[...] = h[tile - 1:tile, :]
    y_ref[...] = (h * _gelu_tanh(gate_ref[...])).astype(y_ref.dtype)

    @pl.when(t == pl.num_programs(1) - 1)
    def _():
        conv_ref[0] = xc_sc[pad + tile - (CONV_W - 1):pad + tile, :]
        h_ref[0] = h[tile - 1:tile, :]


def _prompt_lru(xr, gate, conv_w, conv_b, w_rg, b_rg, w_ig, b_ig, lru_lambda, batch, seq):
    tile = LRU_TILE
    nt = seq // tile
    kern = functools.partial(_prompt_lru_kernel, tile=tile)
    rmap = lambda b, t: (b * nt + t, 0)
    full2 = lambda b, t: (0, 0)
    full3 = lambda b, t: (0, 0, 0)
    bmap = lambda b, t: (b, 0, 0)
    vec = pl.BlockSpec((1, LRU_WIDTH), full2)
    wspec = pl.BlockSpec((LRU_BLOCKS, LRU_BLOCK_W, LRU_BLOCK_W), full3)
    return pl.pallas_call(
        kern,
        grid=(batch, nt),
        in_specs=[pl.BlockSpec((tile, LRU_WIDTH), rmap), pl.BlockSpec((tile, LRU_WIDTH), rmap),
                  pl.BlockSpec((CONV_W, LRU_WIDTH), full2), vec, wspec, vec, wspec, vec, vec],
        out_specs=[pl.BlockSpec((tile, LRU_WIDTH), rmap),
                   pl.BlockSpec((1, CONV_W - 1, LRU_WIDTH), bmap),
                   pl.BlockSpec((1, 1, LRU_WIDTH), bmap)],
        out_shape=[jax.ShapeDtypeStruct((batch * seq, LRU_WIDTH), BF16),
                   jax.ShapeDtypeStruct((batch, CONV_W - 1, LRU_WIDTH), F32),
                   jax.ShapeDtypeStruct((batch, 1, LRU_WIDTH), F32)],
        scratch_shapes=[pltpu.VMEM((tile + SUBLANES, LRU_WIDTH), F32), pltpu.VMEM((1, LRU_WIDTH), F32)],
        compiler_params=_cparams("parallel", "arbitrary"),
        name="prompt_lru",
    )(xr, gate, conv_w, conv_b, w_rg, b_rg, w_ig, b_ig, lru_lambda)


def _sample_lru_kernel(xr_ref, gate_ref, cprev_ref, hprev_ref, cw_ref, cb_ref, wrg_ref, brg_ref,
                       wig_ref, big_ref, lam_ref, y_ref, conv_ref, h_ref, *, steps):
    xcat = [cprev_ref[j] for j in range(CONV_W - 1)] + [xr_ref[t] for t in range(steps)]
    h = hprev_ref[...]
    for t in range(steps):
        conv = cb_ref[...]
        for j in range(CONV_W):
            conv = conv + xcat[t + j] * cw_ref[j:j + 1, :]
        a, bx = _lru_coeffs(conv, wrg_ref, brg_ref[...], wig_ref, big_ref[...], lam_ref[...])
        h = a * h + bx
        y_ref[t] = (h * _gelu_tanh(gate_ref[t])).astype(y_ref.dtype)
    for j in range(CONV_W - 1):
        conv_ref[j] = xcat[steps + j]
    h_ref[...] = h


def _sample_lru(xr_t, gate_t, cprev_t, hprev, conv_w, conv_b, w_rg, b_rg, w_ig, b_ig, lru_lambda):
    steps, nb, _ = xr_t.shape
    kern = functools.partial(_sample_lru_kernel, steps=steps)
    return pl.pallas_call(
        kern,
        out_shape=[jax.ShapeDtypeStruct((steps, nb, LRU_WIDTH), BF16),
                   jax.ShapeDtypeStruct((CONV_W - 1, nb, LRU_WIDTH), F32),
                   jax.ShapeDtypeStruct((nb, LRU_WIDTH), F32)],
        compiler_params=pltpu.CompilerParams(vmem_limit_bytes=VMEM_LIMIT),
        name="sample_lru",
    )(xr_t, gate_t, cprev_t, hprev, conv_w, conv_b, w_rg, b_rg, w_ig, b_ig, lru_lambda)


def _split_bf16(x):
    hi = x.astype(BF16)
    return hi, (x - hi.astype(F32)).astype(BF16)


def _outproj_kernel(a_ref, l_ref, w_ref, x_ref, g_ref, rw_ref, rb_ref, xmid_ref, xp_ref, logit_ref):
    tm = x_ref.shape[0]
    y = jnp.dot(a_ref[...], w_ref[0:ATTN_WIDTH, :], preferred_element_type=F32)
    y = y + jnp.dot(l_ref[...], w_ref[ATTN_WIDTH:, :], preferred_element_type=F32)
    xm = x_ref[...] + y
    xmid_ref[...] = xm
    xn = _rms(xm, g_ref[...])
    xn_hi, xn_lo = _split_bf16(xn)
    bits = _bf16_bits(xn)
    words = bits[:, PACK_W:] | (bits[:, :PACK_W] >> 16)
    xp_ref[...] = words.reshape(tm, PACK_S, LANES)
    rw_hi, rw_lo = _split_bf16(rw_ref[...])
    logits = (jnp.dot(xn_hi, rw_hi, preferred_element_type=F32)
              + jnp.dot(xn_lo, rw_hi, preferred_element_type=F32)
              + jnp.dot(xn_hi, rw_lo, preferred_element_type=F32))
    logit_ref[...] = logits + rb_ref[...]


def _outproj(attn, lru, w_out_bf, x2, g, router_w, router_b, tm):
    n = x2.shape[0]
    row = lambda i: (i, 0)
    full = lambda i: (0, 0)
    return pl.pallas_call(
        _outproj_kernel,
        grid=(n // tm,),
        in_specs=[pl.BlockSpec((tm, ATTN_WIDTH), row), pl.BlockSpec((tm, LRU_WIDTH), row),
                  pl.BlockSpec((D_MODEL, D_MODEL), full, pipeline_mode=pl.Buffered(1)),
                  pl.BlockSpec((tm, D_MODEL), row),
                  pl.BlockSpec((1, D_MODEL), full), pl.BlockSpec((D_MODEL, N_EXPERTS), full),
                  pl.BlockSpec((1, N_EXPERTS), full)],
        out_specs=[pl.BlockSpec((tm, D_MODEL), row), pl.BlockSpec((tm, PACK_S, LANES), lambda i: (i, 0, 0)),
                   pl.BlockSpec((tm, N_EXPERTS), row)],
        out_shape=[jax.ShapeDtypeStruct((n, D_MODEL), F32), jax.ShapeDtypeStruct((n, PACK_S, LANES), U32),
                   jax.ShapeDtypeStruct((n, N_EXPERTS), F32)],
        compiler_params=_cparams("parallel"),
        name="outproj",
    )(attn, lru, w_out_bf, x2, g, router_w, router_b)


def _route_kernel(logit_ref, base_ref, idx_ref, gate_ref, rank_ref, cnt_ref, cnt_sc):
    i = pl.program_id(0)
    tr = logit_ref.shape[0]

    @pl.when(i == 0)
    def _():
        cnt_sc[...] = base_ref[...]

    logits = logit_ref[...]
    lane = lax.broadcasted_iota(I32, logits.shape, 1)
    vals, firsts, hots = [], [], []
    for _ in range(TOP_K):
        mx = jnp.max(logits, axis=-1, keepdims=True)
        first = jnp.min(jnp.where(logits == mx, lane, N_EXPERTS), axis=-1, keepdims=True)
        hot = lane == first
        vals.append(mx)
        firsts.append(first)
        hots.append(hot)
        logits = jnp.where(hot, -jnp.inf, logits)

    member = jnp.zeros(logits.shape, F32)
    for hot in hots:
        member = member + hot.astype(F32)
    r_i = lax.broadcasted_iota(I32, (tr, tr), 0)
    c_i = lax.broadcasted_iota(I32, (tr, tr), 1)
    earlier = jnp.where(c_i < r_i, 1.0, 0.0).astype(BF16)
    before = cnt_sc[...] + jnp.dot(earlier, member.astype(BF16), preferred_element_type=F32)

    exps = [jnp.exp(v - vals[0]) for v in vals]
    denom = exps[0]
    for e in exps[1:]:
        denom = denom + e
    for k in range(TOP_K):
        idx_ref[:, k:k + 1] = firsts[k]
        gate_ref[:, k:k + 1] = exps[k] / denom
        rank = jnp.sum(jnp.where(hots[k], before, 0.0), axis=-1, keepdims=True)
        rank_ref[:, k:k + 1] = rank.astype(I32)
    cnt_sc[...] = cnt_sc[...] + jnp.sum(member, axis=0, keepdims=True)

    @pl.when(i == pl.num_programs(0) - 1)
    def _():
        cnt_ref[...] = cnt_sc[...]


def _route(logits, base_counts, tr):
    n = logits.shape[0]
    row = lambda i: (i, 0)
    full = lambda i: (0, 0)
    k_spec = pl.BlockSpec((tr, TOP_K), row)
    return pl.pallas_call(
        _route_kernel,
        grid=(n // tr,),
        in_specs=[pl.BlockSpec((tr, N_EXPERTS), row), pl.BlockSpec((1, N_EXPERTS), full)],
        out_specs=[k_spec, k_spec, k_spec, pl.BlockSpec((1, N_EXPERTS), full)],
        out_shape=[jax.ShapeDtypeStruct((n, TOP_K), I32), jax.ShapeDtypeStruct((n, TOP_K), F32),
                   jax.ShapeDtypeStruct((n, TOP_K), I32), jax.ShapeDtypeStruct((1, N_EXPERTS), F32)],
        scratch_shapes=[pltpu.VMEM((1, N_EXPERTS), F32)],
        compiler_params=_cparams("arbitrary"),
        name="route",
    )(logits, base_counts)


def _dispatch_kernel(pos_ref, x_ref, xs_in_ref, xs_ref, sem):
    del xs_in_ref
    tm = x_ref.shape[0]

    def issue(r, c):
        for k in range(TOP_K):
            pltpu.make_async_copy(x_ref.at[r], xs_ref.at[pos_ref[r * TOP_K + k]], sem).start(priority=k % 2)
        return c

    lax.fori_loop(0, tm, issue, 0)
    for _ in range(TOP_K):
        pltpu.make_async_copy(x_ref, xs_ref.at[pl.ds(0, tm)], sem).wait()


def _dispatch(pos_flat, x_packed, x_sorted, tm):
    n = x_packed.shape[0]
    return pl.pallas_call(
        _dispatch_kernel,
        grid=(n // tm,),
        in_specs=[pl.BlockSpec((tm * TOP_K,), lambda i: (i,), memory_space=pltpu.SMEM),
                  pl.BlockSpec((tm, PACK_S, LANES), lambda i: (i, 0, 0)),
                  pl.BlockSpec(memory_space=pl.ANY)],
        out_specs=pl.BlockSpec(memory_space=pl.ANY),
        out_shape=jax.ShapeDtypeStruct(x_sorted.shape, x_sorted.dtype),
        scratch_shapes=[pltpu.SemaphoreType.DMA],
        input_output_aliases={2: 0},
        compiler_params=_cparams("arbitrary"),
        name="moe_dispatch",
    )(pos_flat, x_packed, x_sorted)


def _moe_kernel(te_ref, na_ref, x_ref, wg_ref, wu_ref, wd_ref, bg_ref, bu_ref, bd_ref, o_ref,
                xb_sc, act_sc, lo_sc, *, n_f):
    i = pl.program_id(0)
    j = pl.program_id(1)
    tm = x_ref.shape[0]
    tf = act_sc.shape[2]
    last = pl.num_programs(1) - 1
    active = i < na_ref[0]

    @pl.when(jnp.logical_and(active, j == 0))
    def _():
        lo, hi = _unpack_words(x_ref[...].reshape(tm, PACK_W))
        xb_sc[:, :PACK_W] = lo.astype(BF16)
        xb_sc[:, PACK_W:] = hi.astype(BF16)

    @pl.when(jnp.logical_and(active, j < n_f))
    def _():
        x = xb_sc[...]
        glu = jnp.dot(x, wg_ref[0], preferred_element_type=F32) + bg_ref[0]
        lin = jnp.dot(x, wu_ref[0], preferred_element_type=F32) + bu_ref[0]
        glu = jnp.minimum(glu, SWIGLU_LIMIT)
        lin = jnp.clip(lin, -SWIGLU_LIMIT, SWIGLU_LIMIT)
        act_sc[j] = (glu * jax.nn.sigmoid(SWIGLU_ALPHA * glu) * (lin + 1.0)).astype(BF16)

    @pl.when(jnp.logical_and(active, j >= n_f))
    def _():
        y = bd_ref[0] + jnp.dot(act_sc[0], wd_ref[0, 0:tf, :], preferred_element_type=F32)
        for f in range(1, n_f):
            y = y + jnp.dot(act_sc[f], wd_ref[0, f * tf:(f + 1) * tf, :], preferred_element_type=F32)
        bits = _bf16_bits(y)

        @pl.when(j == n_f)
        def _():
            lo_sc[...] = bits >> 16

        @pl.when(j == last)
        def _():
            o_ref[...] = (bits | lo_sc[...]).reshape(tm, PACK_S, LANES)

    @pl.when(jnp.logical_and(jnp.logical_not(active), j == last))
    def _():
        o_ref[...] = jnp.zeros(o_ref.shape, U32)


def _moe_ffn(x_sorted, tile_expert, n_active, w_gate_up, b_gate_up, w_down, b_down):
    n_slots = x_sorted.shape[0]
    tm, tf = MOE_TILE, MOE_FF_TILE
    n_tiles = n_slots // tm
    n_f = D_FF // tf
    n_d = D_MODEL // PACK_W
    kern = functools.partial(_moe_kernel, n_f=n_f)

    def fidx(i, j, na):
        return jnp.where(i < na[0], jnp.minimum(j, n_f - 1), n_f - 1)

    def down_block(i, j, te, na):
        in_down = jnp.logical_and(i < na[0], j >= n_f)
        e = jnp.where(in_down, te[i], te[jnp.maximum(i - 1, 0)])
        return e, 0, jnp.where(in_down, j - n_f, n_d - 1)

    def rowidx(i, na):
        return jnp.minimum(i, na[0] - 1)

    grid_spec = pltpu.PrefetchScalarGridSpec(
        num_scalar_prefetch=2,
        grid=(n_tiles, n_f + n_d),
        in_specs=[pl.BlockSpec((tm, PACK_S, LANES), lambda i, j, te, na: (rowidx(i, na), 0, 0)),
                  pl.BlockSpec((1, D_MODEL, tf), lambda i, j, te, na: (te[i], 0, fidx(i, j, na))),
                  pl.BlockSpec((1, D_MODEL, tf), lambda i, j, te, na: (te[i], 0, n_f + fidx(i, j, na))),
                  pl.BlockSpec((1, D_FF, PACK_W), down_block),
                  pl.BlockSpec((1, 1, tf), lambda i, j, te, na: (te[i], 0, fidx(i, j, na))),
                  pl.BlockSpec((1, 1, tf), lambda i, j, te, na: (te[i], 0, n_f + fidx(i, j, na))),
                  pl.BlockSpec((1, 1, PACK_W), down_block)],
        out_specs=pl.BlockSpec((tm, PACK_S, LANES), lambda i, j, te, na: (i, 0, 0)),
        scratch_shapes=[pltpu.VMEM((tm, D_MODEL), BF16), pltpu.VMEM((n_f, tm, tf), BF16),
                        pltpu.VMEM((tm, PACK_W), U32)],
    )
    bgu3 = b_gate_up.reshape(N_EXPERTS, 1, 2 * D_FF)
    bd3 = b_down.reshape(N_EXPERTS, 1, D_MODEL)
    return pl.pallas_call(
        kern,
        grid_spec=grid_spec,
        out_shape=jax.ShapeDtypeStruct((n_slots, PACK_S, LANES), U32),
        compiler_params=_cparams("arbitrary", "arbitrary"),
        name="moe_ffn",
    )(tile_expert, n_active, x_sorted, w_gate_up, w_gate_up, w_down, bgu3, bgu3, bd3)


def _combine_kernel(pos_ref, pos_next_ref, gate_ref, x_ref, g_ref, y_hbm, o_ref, ybuf, sems):
    i = pl.program_id(0)
    n = pl.num_programs(0)
    tm = x_ref.shape[0]
    slot = i % 2
    rows_per_group = min(COMBINE_GROUP, tm)

    def issue_group(p_ref, buf_slot, r0):
        for rr in range(rows_per_group):
            r = r0 + rr
            for k in range(TOP_K):
                pltpu.make_async_copy(y_hbm.at[p_ref[r * TOP_K + k]], ybuf.at[buf_slot, k, r],
                                      sems.at[buf_slot]).start()

    def wait_slot(buf_slot):
        for k in range(TOP_K):
            pltpu.make_async_copy(y_hbm.at[pl.ds(0, tm)], ybuf.at[buf_slot, k], sems.at[buf_slot]).wait()

    @pl.when(i == 0)
    def _():
        def first(gi, c):
            issue_group(pos_ref, 0, gi * rows_per_group)
            return c
        lax.fori_loop(0, tm // rows_per_group, first, 0)

    wait_slot(slot)

    def group(gi, c):
        r0 = pl.multiple_of(gi * rows_per_group, rows_per_group)
        issue_group(pos_next_ref, 1 - slot, r0)
        rows = pl.ds(r0, rows_per_group)
        gates = gate_ref[rows, :]
        x = x_ref[rows, :]
        out_lo, out_hi = x[:, :PACK_W], x[:, PACK_W:]
        for k in range(TOP_K):
            lo, hi = _unpack_words(ybuf[slot, k, rows].reshape(rows_per_group, PACK_W))
            out_lo = out_lo + gates[:, k:k + 1] * lo
            out_hi = out_hi + gates[:, k:k + 1] * hi
        o_ref[rows, :] = _rms(jnp.concatenate([out_lo, out_hi], axis=1), g_ref[...])
        return c

    lax.fori_loop(0, tm // rows_per_group, group, 0)

    @pl.when(i == n - 1)
    def _():
        wait_slot(1 - slot)


def _combine(pos_flat, gates, x_mid, g, y_sorted, tm):
    n = x_mid.shape[0]
    steps = n // tm
    row = lambda i: (i, 0)
    return pl.pallas_call(
        _combine_kernel,
        grid=(steps,),
        in_specs=[pl.BlockSpec((tm * TOP_K,), lambda i: (i,), memory_space=pltpu.SMEM),
                  pl.BlockSpec((tm * TOP_K,), lambda i: (jnp.minimum(i + 1, steps - 1),),
                               memory_space=pltpu.SMEM),
                  pl.BlockSpec((tm, TOP_K), row),
                  pl.BlockSpec((tm, D_MODEL), row),
                  pl.BlockSpec((1, D_MODEL), lambda i: (0, 0)),
                  pl.BlockSpec(memory_space=pl.ANY)],
        out_specs=pl.BlockSpec((tm, D_MODEL), row),
        out_shape=jax.ShapeDtypeStruct((n, D_MODEL), F32),
        scratch_shapes=[pltpu.VMEM((2, TOP_K, tm, PACK_S, LANES), U32), pltpu.SemaphoreType.DMA((2,))],
        compiler_params=_cparams("arbitrary"),
        name="moe_combine",
    )(pos_flat, pos_flat, gates, x_mid, g, y_sorted)


def _bias_of_distance(rel_table, n):
    nf = jnp.maximum(n, 1).astype(F32)
    large = REL_EXACT + (jnp.log(nf / REL_EXACT) / math.log(REL_MAX_DIST / REL_EXACT)
                         * (REL_BUCKETS - REL_EXACT)).astype(I32)
    bucket = jnp.where(n < REL_EXACT, n, jnp.minimum(large, REL_BUCKETS - 1))
    hot = bucket[..., None, None] == jnp.arange(REL_BUCKETS, dtype=I32)[:, None]
    return jnp.sum(jnp.where(hot, rel_table.astype(F32), 0.0), axis=-2)


def _prompt_bias_tables(rel_table, tile):
    far = rel_table[REL_BUCKETS - 1].astype(F32)
    length = 3 * tile - 1
    d = jnp.arange(length) - (tile - 1)
    b = (_bias_of_distance(rel_table, jnp.maximum(d, 0)) - far) * LOG2E
    b = jnp.where((d >= 0)[:, None], b, NEG_INF).T
    flat = jnp.tile(b, (1, tile + 1))[:, :tile * (length + 1)]
    shifted = flat.reshape(N_HEADS, tile, length + 1)
    full = shifted[:, ::-1, :2 * tile]
    return full[:, :, :tile], full[:, :, tile:]


def _sample_bias_tables(rel_table, dec_seq, past_len, page_size):
    heads = jnp.arange(N_HEADS)
    same_head = heads[:, None] == heads[None, :]

    def table(dist, valid):
        b = _bias_of_distance(rel_table, jnp.maximum(dist, 0))
        b = jnp.transpose(b, (0, 2, 1))
        full = jnp.where(same_head[None, :, None, :] & valid[:, None, :, None],
                         b[:, :, :, None], NEG_INF)
        full = full.reshape(dec_seq * N_HEADS, -1)
        return jnp.concatenate([full, full], axis=0)

    qpos = past_len + jnp.arange(dec_seq)
    span = PAGES_PER_STEP * page_size
    always = jnp.ones((dec_seq, span), bool)
    k_last = past_len - span + jnp.arange(span)
    tab_far = table(jnp.full((dec_seq, span), REL_MAX_DIST), always)
    tab_last = table(qpos[:, None] - k_last[None, :], always)
    n_self = HEAD_W // N_HEADS
    tk = jnp.arange(n_self)
    dist = jnp.arange(dec_seq)[:, None] - tk[None, :]
    tab_self = table(dist, (dist >= 0) & (tk[None, :] < dec_seq))
    return jnp.stack([tab_far, tab_last]), tab_self


def _tile_plan(counts, n_assign, tm):
    counts = counts.reshape(N_EXPERTS).astype(I32)
    padded = (counts + tm - 1) // tm * tm
    pad_end = jnp.cumsum(padded)
    pad_start = pad_end - padded
    n_tiles = n_assign // tm + N_EXPERTS
    n_active = (pad_end[-1] // tm).astype(I32)
    tile_start = jnp.minimum(jnp.arange(n_tiles, dtype=I32), n_active - 1) * tm
    tile_expert = jnp.minimum(jnp.sum((pad_end[None, :] <= tile_start[:, None]).astype(I32), axis=1),
                              N_EXPERTS - 1)
    return pad_start, tile_expert, n_active.reshape(1), n_tiles


def _slots(pad_start, idx, rank):
    hot = idx[..., None] == jnp.arange(N_EXPERTS, dtype=I32)
    return (jnp.sum(jnp.where(hot, pad_start, 0), axis=-1) + rank).reshape(-1)


def kernel(x_prompt, x_sample, cache_k, cache_v, state_conv, state_h, page_table, rel_bias, norm_mix_g, w_in, lam_q1, lam_k1, lam_q2, lam_k2, subln_g, conv_w, conv_b, w_rg, b_rg, w_ig, b_ig, lru_lambda, w_out, norm_ffn_g, router_w, router_b, w_gate_up, b_gate_up, w_down, b_down, norm_final_g):
    depth = w_in.shape[0]
    assert depth == 1
    layer = 0
    bp, seq, _ = x_prompt.shape
    bs, dec_seq, _ = x_sample.shape
    n_pool, page_size = cache_k.shape[1], cache_k.shape[2]
    past_len = page_table.shape[1] * page_size
    n_p, n_s = bp * seq, bs * dec_seq
    assert ATTN_TILE >= REL_MAX_DIST and PAGES_PER_STEP * page_size >= REL_MAX_DIST

    lam_init = 0.8 - 0.6 * math.exp(-0.3 * layer)
    out_scale = 1.0 - lam_init
    lam = (jnp.exp(jnp.sum(lam_q1[layer] * lam_k1[layer])) - jnp.exp(jnp.sum(lam_q2[layer] * lam_k2[layer]))
           + lam_init).reshape(1).astype(F32)

    g_mix = norm_mix_g[layer].reshape(1, D_MODEL)
    g_ffn = norm_ffn_g[layer].reshape(1, D_MODEL)
    g_fin = norm_final_g.reshape(1, D_MODEL)
    g_sub = subln_g[layer].reshape(1, HEAD_W)
    w_in_bf = w_in[layer].astype(BF16)
    w_out_bf = w_out[layer].astype(BF16)
    lru_args = (conv_w[layer], conv_b[layer].reshape(1, LRU_WIDTH),
                w_rg[layer].astype(BF16), b_rg[layer].reshape(1, LRU_WIDTH),
                w_ig[layer].astype(BF16), b_ig[layer].reshape(1, LRU_WIDTH),
                lru_lambda[layer].reshape(1, LRU_WIDTH))
    rw = router_w[layer]
    rb = router_b[layer].reshape(1, N_EXPERTS)

    xp2 = x_prompt.reshape(n_p, D_MODEL)
    xs2 = x_sample.reshape(n_s, D_MODEL)

    q_p, k_p, v_p, kb_p, vb_p, xr_p, gate_p = _inproj(xp2, g_mix, w_in_bf, ROW_TILE)
    t0, t1 = _prompt_bias_tables(rel_bias, ATTN_TILE)
    attn_p, wgu_bf, wd_bf = _prompt_attention(
        q_p, kb_p, vb_p, t0, t1, lam, g_sub,
        w_gate_up[layer].reshape(N_EXPERTS * D_MODEL, 2 * D_FF), w_down[layer].reshape(N_EXPERTS * D_FF, D_MODEL),
        bp, seq, out_scale)
    lru_p, conv_p, h_p = _prompt_lru(xr_p, gate_p, *lru_args, bp, seq)
    xmid_p, xpk_p, logit_p = _outproj(attn_p, lru_p, w_out_bf, xp2, g_ffn, rw, rb, ROW_TILE)

    q_s, k_s, v_s, _, _, xr_s, gate_s = _inproj(xs2, g_mix, w_in_bf, n_s)
    tab, tab_self = _sample_bias_tables(rel_bias, dec_seq, past_len, page_size)
    rows = dec_seq * N_HEADS
    pad_rows = HEAD_W - rows

    def self_rows(a):
        a3 = a.reshape(bs, rows, HEAD_W)
        return jnp.concatenate([a3, jnp.zeros((bs, pad_rows, HEAD_W), a.dtype)], axis=1)

    attn_s = _sample_attention(
        q_s.reshape(bs, rows, HEAD_W),
        cache_k[layer].reshape(n_pool, page_size * N_HEADS, HEAD_W),
        cache_v[layer].reshape(n_pool, page_size * N_HEADS, HEAD_W),
        page_table, tab, self_rows(k_s), self_rows(v_s), tab_self, lam, g_sub, out_scale)
    attn_s = attn_s.reshape(n_s, ATTN_WIDTH)

    def tmajor(a):
        return jnp.transpose(a.reshape(bs, dec_seq, LRU_WIDTH), (1, 0, 2))

    lru_s_t, conv_s_t, h_s = _sample_lru(tmajor(xr_s), tmajor(gate_s),
                                         jnp.transpose(state_conv[layer], (1, 0, 2)), state_h[layer],
                                         *lru_args)
    lru_s = jnp.transpose(lru_s_t, (1, 0, 2)).reshape(n_s, LRU_WIDTH)
    conv_s = jnp.transpose(conv_s_t, (1, 0, 2))
    xmid_s, xpk_s, logit_s = _outproj(attn_s, lru_s, w_out_bf, xs2, g_ffn, rw, rb, n_s)

    idx_p, gates_p, rank_p, cnt_p = _route(logit_p, jnp.zeros((1, N_EXPERTS), F32), ROW_TILE)
    idx_s, gates_s, rank_s, cnt = _route(logit_s, cnt_p, n_s)
    n_assign = (n_p + n_s) * TOP_K
    pad_start, tile_expert, n_active, n_tiles = _tile_plan(cnt, n_assign, MOE_TILE)
    pos_p = _slots(pad_start, idx_p, rank_p)
    pos_s = _slots(pad_start, idx_s, rank_s)
    x_sorted = jnp.zeros((n_tiles * MOE_TILE, PACK_S, LANES), U32)
    x_sorted = _dispatch(pos_p, xpk_p, x_sorted, ROW_TILE)
    x_sorted = _dispatch(pos_s, xpk_s, x_sorted, n_s)
    y_sorted = _moe_ffn(x_sorted, tile_expert, n_active, wgu_bf.reshape(N_EXPERTS, D_MODEL, 2 * D_FF),
                        b_gate_up[layer], wd_bf.reshape(N_EXPERTS, D_FF, D_MODEL), b_down[layer])
    y_p = _combine(pos_p, gates_p, xmid_p, g_fin, y_sorted, COMBINE_TILE)
    y_s = _combine(pos_s, gates_s, xmid_s, g_fin, y_sorted, n_s)

    kv_p = (depth, bp, seq, N_HEADS, HEAD_W)
    kv_s = (depth, bs, dec_seq, N_HEADS, HEAD_W)
    return (y_p.reshape(bp, seq, D_MODEL), y_s.reshape(bs, dec_seq, D_MODEL),
            k_p.reshape(kv_p), v_p.reshape(kv_p),
            conv_p.reshape(depth, bp, CONV_W - 1, LRU_WIDTH), h_p.reshape(depth, bp, LRU_WIDTH),
            k_s.reshape(kv_s), v_s.reshape(kv_s),
            conv_s.reshape(depth, bs, CONV_W - 1, LRU_WIDTH), h_s.reshape(depth, bs, LRU_WIDTH))
```
